```python
import math
import jax, jax.numpy as jnp
from jax import lax
import numpy as np

D_MODEL = 2048
BATCH = 8
SEQ = 2048
DEPTH = 2

D_FF = 5632
NORM_EPS = 1e-6
NEG_INF = -1e30
FORCE = 1e6
LOG_FLOOR = 1e-30

GDN_HEADS = 8
GDN_DK = 128
GDN_DV = 128
GDN_CONV = 4
GDN_CHUNK = 64
GDN_QK = GDN_HEADS * GDN_DK
GDN_V = GDN_HEADS * GDN_DV

NSA_HEADS = 8
NSA_GROUPS = 2
NSA_DH = 128
NSA_Q = NSA_HEADS * NSA_DH
NSA_KV = NSA_GROUPS * NSA_DH
CMP_LEN = 32
CMP_STRIDE = 16
SLC_LEN = 64
SLC_TOPN = 8
WIN_LEN = 512
WIN_Q_BLOCK = 128
SLC_Q_BLOCK = 64

HG_HEADS = 8
HG_DK = 128
HG_DV = 128
HG_CHUNK = 64
HG_K = HG_HEADS * HG_DK
HG_V = HG_HEADS * HG_DV

N_BRANCH = 3
IN_WIDTHS = (2 * GDN_QK + GDN_V, GDN_HEADS, GDN_HEADS, GDN_V,
             NSA_Q, 6 * NSA_KV, 3 * NSA_HEADS,
             HG_K, HG_K, HG_V, HG_V,
             N_BRANCH * D_MODEL)
N_IN = sum(IN_WIDTHS)

kernel_name = 'hybrid_gdn_nsa_hgrn2_macaron'


def rmsnorm(x, g, eps=NORM_EPS):
    xf = x.astype(jnp.float32)
    y = xf * lax.rsqrt(jnp.mean(xf * xf, axis=-1, keepdims=True) + eps)
    return (y * g.astype(jnp.float32)).astype(x.dtype)


def l2norm(x, eps=1e-6):
    return x * lax.rsqrt(jnp.sum(x * x, axis=-1, keepdims=True) + eps)


def swiglu(h, wg, wu, wd):
    return (jax.nn.silu(h @ wg) * (h @ wu)) @ wd


def split_cols(u, widths):
    offs = np.cumsum(np.array(widths))[:-1].tolist()
    return jnp.split(u, offs, axis=-1)


def causal_dwconv(x, w):
    k, c = w.shape
    return lax.conv_general_dilated(
        x, w.astype(x.dtype)[:, None, :], window_strides=(1,), padding=[(k - 1, 0)],
        dimension_numbers=('NWC', 'WIO', 'NWC'), feature_group_count=c)


def gated_delta_rule(q, k, v, beta, g):
    B, H, S, dk = q.shape
    dv = v.shape[-1]
    C = GDN_CHUNK
    n = S // C
    q, k, v = (t.reshape(B, H, n, C, t.shape[-1]) for t in (q, k, v))
    beta = beta.reshape(B, H, n, C)
    gc = jnp.cumsum(g.reshape(B, H, n, C), axis=-1)
    causal = jnp.tril(jnp.ones((C, C), bool))
    strict = jnp.tril(jnp.ones((C, C), bool), -1)
    diff = gc[..., :, None] - gc[..., None, :]
    decay = jnp.where(causal, jnp.exp(jnp.where(causal, diff, 0.0)), 0.0)
    kb = k * beta[..., None]
    m = jnp.where(strict, jnp.einsum('bhntd,bhnsd->bhnts', kb, k) * decay, 0.0)
    eye = jnp.eye(C, dtype=jnp.float32)
    t_inv = lax.linalg.triangular_solve(eye + m, jnp.broadcast_to(eye, m.shape),
                                        left_side=True, lower=True, unit_diagonal=True)
    u = t_inv @ (v * beta[..., None])
    w = t_inv @ (kb * jnp.exp(gc)[..., None])
    attn = jnp.where(causal, jnp.einsum('bhntd,bhnsd->bhnts', q, k) * decay, 0.0)

    def step(state, inp):
        q_i, k_i, u_i, w_i, a_i, gc_i = inp
        v_new = u_i - w_i @ state
        o = (q_i * jnp.exp(gc_i)[..., None]) @ state + a_i @ v_new
        g_last = gc_i[..., -1]
        state = state * jnp.exp(g_last)[..., None, None] + jnp.einsum(
            'bhcd,bhce->bhde', k_i * jnp.exp(g_last[..., None] - gc_i)[..., None], v_new)
        return state, o

    xs = tuple(jnp.moveaxis(t, 2, 0) for t in (q, k, u, w, attn, gc))
    _, o = lax.scan(step, jnp.zeros((B, H, dk, dv), jnp.float32), xs)
    return jnp.moveaxis(o, 0, 2).reshape(B, H, S, dv)


def gdn_mixer(qkv, beta_pre, a_pre, gate, conv_w, a_log, dt_bias, norm_g):
    dt = qkv.dtype
    B, S, _ = qkv.shape
    f32 = jnp.float32
    qkv = jax.nn.silu(causal_dwconv(qkv, conv_w)).astype(f32)
    q, k, v = jnp.split(qkv, [GDN_QK, 2 * GDN_QK], axis=-1)
    heads = lambda t, d: t.reshape(B, S, GDN_HEADS, d).transpose(0, 2, 1, 3)
    q = l2norm(heads(q, GDN_DK)) * GDN_DK ** -0.5
    k = l2norm(heads(k, GDN_DK))
    v = heads(v, GDN_DV)
    beta = jax.nn.sigmoid(beta_pre.astype(f32)).transpose(0, 2, 1)
    g = (-jnp.exp(a_log.astype(f32)) * jax.nn.softplus(a_pre.astype(f32) + dt_bias.astype(f32))).transpose(0, 2, 1)
    o = gated_delta_rule(q, k, v, beta, g).transpose(0, 2, 1, 3)
    o = rmsnorm(o, norm_g) * jax.nn.silu(gate.astype(f32).reshape(B, S, GDN_HEADS, GDN_DV))
    return o.reshape(B, S, GDN_V).astype(dt)


def nsa_mixer(q, kv, gate_pre, pe_k, pe_v, wk1, wk2, wv1, wv2):
    dt = q.dtype
    B, S, _ = q.shape
    G, R, dh = NSA_GROUPS, NSA_HEADS // NSA_GROUPS, NSA_DH
    f32 = jnp.float32
    q = q.astype(f32).reshape(B, S, G, R, dh).transpose(0, 2, 3, 1, 4) * dh ** -0.5
    kv = kv.astype(f32).reshape(B, S, 6, G, dh).transpose(2, 0, 3, 1, 4)
    k_cmp, v_cmp, k_slc, v_slc, k_win, v_win = (kv[i] for i in range(6))
    pos = jnp.arange(S)

    n_cmp = (S - CMP_LEN) // CMP_STRIDE + 1
    cmp_start = jnp.arange(n_cmp) * CMP_STRIDE
    blk = cmp_start[:, None] + jnp.arange(CMP_LEN)

    def compress(t, pe, w1, w2):
        tb = (t[:, :, blk] + pe).reshape(B, G, n_cmp, CMP_LEN * dh)
        return jax.nn.silu(tb @ w1) @ w2

    kc = compress(k_cmp, pe_k, wk1, wk2)
    vc = compress(v_cmp, pe_v, wv1, wv2)
    valid_c = (cmp_start[None, :] + CMP_LEN - 1) <= pos[:, None]
    s_c = jnp.einsum('bgrtd,bgnd->bgrtn', q, kc)
    p_c = jax.nn.softmax(jnp.where(valid_c, s_c, NEG_INF), axis=-1) * valid_c
    o_cmp = jnp.einsum('bgrtn,bgnd->bgrtd', p_c, vc)

    n_slc = S // SLC_LEN
    top_n = min(SLC_TOPN, n_slc)
    slc_start = jnp.arange(n_slc) * SLC_LEN
    overlap = ((cmp_start[:, None] < slc_start[None, :] + SLC_LEN)
               & (cmp_start[:, None] + CMP_LEN > slc_start[None, :])).astype(f32)
    imp = jnp.einsum('bgrtn,nj->bgtj', p_c, overlap)
    blk_id = jnp.arange(n_slc)[None, :]
    cur = (pos // SLC_LEN)[:, None]
    forced = (blk_id == 0) | (blk_id == cur) | (blk_id == cur - 1)
    imp = jnp.where(blk_id > cur, -FORCE, jnp.where(forced, FORCE, imp))
    _, sel = lax.top_k(imp, top_n)
    kb = k_slc.reshape(B, G, n_slc, SLC_LEN, dh)
    vb = v_slc.reshape(B, G, n_slc, SLC_LEN, dh)
    bi = jnp.arange(B)[:, None, None, None]
    gi = jnp.arange(G)[None, :, None, None]
    tq = SLC_Q_BLOCK
    nq = S // tq

    def slc_block(args):
        q_blk, sel_blk, t_blk = args
        kg = kb[bi, gi, sel_blk]
        vg = vb[bi, gi, sel_blk]
        s = jnp.einsum('bgrtd,bgtnld->bgrtnl', q_blk, kg)
        kpos = sel_blk[..., None] * SLC_LEN + jnp.arange(SLC_LEN)
        ok = (kpos <= t_blk[:, None, None])[:, :, None]
        s = jnp.where(ok, s, NEG_INF).reshape(B, G, R, tq, top_n * SLC_LEN)
        p = jax.nn.softmax(s, axis=-1).reshape(B, G, R, tq, top_n, SLC_LEN)
        return jnp.einsum('bgrtnl,bgtnld->bgrtd', p, vg)

    o_slc = lax.map(slc_block, (jnp.moveaxis(q.reshape(B, G, R, nq, tq, dh), 3, 0),
                                jnp.moveaxis(sel.reshape(B, G, nq, tq, top_n), 2, 0),
                                pos.reshape(nq, tq)))
    o_slc = jnp.moveaxis(o_slc, 0, 3).reshape(B, G, R, S, dh)

    tw = WIN_Q_BLOCK
    nw = S // tw
    pad = ((0, 0), (0, 0), (WIN_LEN, 0), (0, 0))
    kp = jnp.pad(k_win, pad)
    vp = jnp.pad(v_win, pad)

    def win_block(args):
        q_blk, i = args
        s0 = i * tw
        kw = lax.dynamic_slice_in_dim(kp, s0, WIN_LEN + tw, axis=2)
        vw = lax.dynamic_slice_in_dim(vp, s0, WIN_LEN + tw, axis=2)
        t = s0 + jnp.arange(tw)[:, None]
        p = s0 - WIN_LEN + jnp.arange(WIN_LEN + tw)[None, :]
        ok = (p <= t) & (p > t - WIN_LEN) & (p >= 0)
        s = jnp.einsum('bgrtd,bgkd->bgrtk', q_blk, kw)
        a = jax.nn.softmax(jnp.where(ok, s, NEG_INF), axis=-1)
        return jnp.einsum('bgrtk,bgkd->bgrtd', a, vw)

    o_win = lax.map(win_block, (jnp.moveaxis(q.reshape(B, G, R, nw, tw, dh), 3, 0), jnp.arange(nw)))
    o_win = jnp.moveaxis(o_win, 0, 3).reshape(B, G, R, S, dh)

    gates = jax.nn.sigmoid(gate_pre.astype(f32)).reshape(B, S, G, R, 3).transpose(4, 0, 2, 3, 1)[..., None]
    o = gates[0] * o_cmp + gates[1] * o_slc + gates[2] * o_win
    return o.transpose(0, 3, 1, 2, 4).reshape(B, S, NSA_Q).astype(dt)


def hgrn2_recurrence(q, k, v, logf):
    B, H, S, dk = q.shape
    dv = v.shape[-1]
    C = HG_CHUNK
    n = S // C
    to_chunks = lambda t: jnp.moveaxis(t.reshape(B, H, n, C, t.shape[-1]), 2, 0)
    b = jnp.cumsum(logf.reshape(B, H, n, C, dk), axis=3)
    xs = (to_chunks(q), to_chunks(k), to_chunks(v), jnp.moveaxis(b, 2, 0))
    causal = jnp.tril(jnp.ones((C, C), bool))[:, :, None]

    def step(state, inp):
        q_i, k_i, v_i, b_i = inp
        diff = b_i[:, :, :, None, :] - b_i[:, :, None, :, :]
        dec = jnp.where(causal, jnp.exp(jnp.where(causal, diff, 0.0)), 0.0)
        a = jnp.einsum('bhtd,bhsd,bhtsd->bhts', q_i, k_i, dec)
        o = jnp.einsum('bhtd,bhde->bhte', q_i * jnp.exp(b_i), state) + jnp.einsum('bhts,bhse->bhte', a, v_i)
        b_last = b_i[:, :, -1:, :]
        state = state * jnp.exp(b_last[:, :, 0, :, None]) + jnp.einsum(
            'bhsd,bhse->bhde', k_i * jnp.exp(b_last - b_i), v_i)
        return state, o

    _, o = lax.scan(step, jnp.zeros((B, H, dk, dv), jnp.float32), xs)
    return jnp.moveaxis(o, 0, 2).reshape(B, H, S, dv)


def hgrn2_mixer(q, f_pre, i, g, lb, norm_g):
    dt = q.dtype
    B, S, _ = q.shape
    f32 = jnp.float32
    heads = lambda t, d: t.astype(f32).reshape(B, S, HG_HEADS, d).transpose(0, 2, 1, 3)
    q = jax.nn.silu(heads(q, HG_DK))
    lb = lb.astype(f32).reshape(HG_HEADS, 1, HG_DK)
    f = lb + (1.0 - lb) * jax.nn.sigmoid(heads(f_pre, HG_DK))
    logf = jnp.log(jnp.maximum(f, LOG_FLOOR))
    k = 1.0 - f
    v = heads(i, HG_DV)
    o = hgrn2_recurrence(q, k, v, logf).transpose(0, 2, 1, 3).reshape(B, S, HG_V)
    o = rmsnorm(o, norm_g) * jax.nn.silu(g.astype(f32))
    return o.astype(dt)


def setup_inputs(seed: int = 0) -> dict:
    key = jax.random.key(seed)
    ks = iter(jax.random.split(key, 40))
    f32 = jnp.float32
    L = DEPTH

    def nrm(shape, scale):
        return jax.random.normal(next(ks), shape, f32) * scale

    def gain(shape):
        return 1.0 + nrm(shape, 0.02)

    x = nrm((BATCH, SEQ, D_MODEL), 1.0)
    ffn1_norm = gain((L, D_MODEL))
    ffn1_w_gate = nrm((L, D_MODEL, D_FF), D_MODEL ** -0.5)
    ffn1_w_up = nrm((L, D_MODEL, D_FF), D_MODEL ** -0.5)
    ffn1_w_down = nrm((L, D_FF, D_MODEL), D_FF ** -0.5)
    mix_norm = gain((L, D_MODEL))
    w_in = nrm((L, D_MODEL, N_IN), D_MODEL ** -0.5)
    gdn_conv = nrm((L, GDN_CONV, 2 * GDN_QK + GDN_V), GDN_CONV ** -0.5)
    gdn_a_log = jnp.log(jax.random.uniform(next(ks), (L, GDN_HEADS), f32, 1.0, 16.0))
    dt0 = jnp.exp(jax.random.uniform(next(ks), (L, GDN_HEADS), f32, math.log(1e-3), math.log(1e-1)))
    gdn_dt_bias = dt0 + jnp.log(-jnp.expm1(-dt0))
    gdn_out_norm = gain((L, GDN_DV))
    nsa_cmp_pe_k = nrm((L, CMP_LEN, NSA_DH), 0.1)
    nsa_cmp_pe_v = nrm((L, CMP_LEN, NSA_DH), 0.1)
    nsa_cmp_k_w1 = nrm((L, CMP_LEN * NSA_DH, NSA_DH), (CMP_LEN * NSA_DH) ** -0.5)
    nsa_cmp_k_w2 = nrm((L, NSA_DH, NSA_DH), NSA_DH ** -0.5)
    nsa_cmp_v_w1 = nrm((L, CMP_LEN * NSA_DH, NSA_DH), (CMP_LEN * NSA_DH) ** -0.5)
    nsa_cmp_v_w2 = nrm((L, NSA_DH, NSA_DH), NSA_DH ** -0.5)
    hgrn_lb_logits = nrm((L, HG_K), 1.0)
    hgrn_out_norm = gain((L, HG_V))
    w_proj_a = nrm((L, GDN_V, D_MODEL), GDN_V ** -0.5)
    w_proj_b = nrm((L, NSA_Q, D_MODEL), NSA_Q ** -0.5)
    w_proj_c = nrm((L, HG_V, D_MODEL), HG_V ** -0.5)
    w_out = nrm((L, D_MODEL, D_MODEL), D_MODEL ** -0.5)
    ffn2_norm = gain((L, D_MODEL))
    ffn2_w_gate = nrm((L, D_MODEL, D_FF), D_MODEL ** -0.5)
    ffn2_w_up = nrm((L, D_MODEL, D_FF), D_MODEL ** -0.5)
    ffn2_w_down = nrm((L, D_FF, D_MODEL), D_FF ** -0.5)
    final_norm = gain((D_MODEL,))
    return {'x': x, 'ffn1_norm': ffn1_norm, 'ffn1_w_gate': ffn1_w_gate, 'ffn1_w_up': ffn1_w_up,
            'ffn1_w_down': ffn1_w_down, 'mix_norm': mix_norm, 'w_in': w_in, 'gdn_conv': gdn_conv,
            'gdn_a_log': gdn_a_log, 'gdn_dt_bias': gdn_dt_bias, 'gdn_out_norm': gdn_out_norm,
            'nsa_cmp_pe_k': nsa_cmp_pe_k, 'nsa_cmp_pe_v': nsa_cmp_pe_v,
            'nsa_cmp_k_w1': nsa_cmp_k_w1, 'nsa_cmp_k_w2': nsa_cmp_k_w2,
            'nsa_cmp_v_w1': nsa_cmp_v_w1, 'nsa_cmp_v_w2': nsa_cmp_v_w2,
            'hgrn_lb_logits': hgrn_lb_logits, 'hgrn_out_norm': hgrn_out_norm,
            'w_proj_a': w_proj_a, 'w_proj_b': w_proj_b, 'w_proj_c': w_proj_c, 'w_out': w_out,
            'ffn2_norm': ffn2_norm, 'ffn2_w_gate': ffn2_w_gate, 'ffn2_w_up': ffn2_w_up,
            'ffn2_w_down': ffn2_w_down, 'final_norm': final_norm}


def reference(x, ffn1_norm, ffn1_w_gate, ffn1_w_up, ffn1_w_down, mix_norm, w_in, gdn_conv,
              gdn_a_log, gdn_dt_bias, gdn_out_norm, nsa_cmp_pe_k, nsa_cmp_pe_v,
              nsa_cmp_k_w1, nsa_cmp_k_w2, nsa_cmp_v_w1, nsa_cmp_v_w2,
              hgrn_lb_logits, hgrn_out_norm, w_proj_a, w_proj_b, w_proj_c, w_out,
              ffn2_norm, ffn2_w_gate, ffn2_w_up, ffn2_w_down, final_norm):
    B, S, _ = x.shape
    lb_p = jax.nn.softmax(hgrn_lb_logits.astype(jnp.float32), axis=0)
    lower_bounds = jnp.cumsum(lb_p, axis=0) - lb_p[0]
    for l in range(DEPTH):
        h = rmsnorm(x, ffn1_norm[l])
        x = x + 0.5 * swiglu(h, ffn1_w_gate[l], ffn1_w_up[l], ffn1_w_down[l])

        h = rmsnorm(x, mix_norm[l])
        u = h @ w_in[l]
        (gdn_qkv, gdn_beta, gdn_a, gdn_gate, nsa_q, nsa_kv, nsa_gate,
         hg_q, hg_f, hg_i, hg_g, merge_pre) = split_cols(u, IN_WIDTHS)
        y_a = gdn_mixer(gdn_qkv, gdn_beta, gdn_a, gdn_gate, gdn_conv[l], gdn_a_log[l],
                        gdn_dt_bias[l], gdn_out_norm[l])
        y_b = nsa_mixer(nsa_q, nsa_kv, nsa_gate, nsa_cmp_pe_k[l], nsa_cmp_pe_v[l],
                        nsa_cmp_k_w1[l], nsa_cmp_k_w2[l], nsa_cmp_v_w1[l], nsa_cmp_v_w2[l])
        y_c = hgrn2_mixer(hg_q, hg_f, hg_i, hg_g, lower_bounds[l], hgrn_out_norm[l])
        gates = jax.nn.sigmoid(merge_pre.astype(jnp.float32)).astype(x.dtype).reshape(B, S, N_BRANCH, D_MODEL)
        merged = (gates[:, :, 0] * (y_a @ w_proj_a[l])
                  + gates[:, :, 1] * (y_b @ w_proj_b[l])
                  + gates[:, :, 2] * (y_c @ w_proj_c[l]))
        x = x + merged @ w_out[l]

        h = rmsnorm(x, ffn2_norm[l])
        x = x + 0.5 * swiglu(h, ffn2_w_gate[l], ffn2_w_up[l], ffn2_w_down[l])
    return rmsnorm(x, final_norm)
```

```python
import functools

import jax
import jax.numpy as jnp
from jax import lax
from jax.experimental import pallas as pl
from jax.experimental.pallas import tpu as pltpu

F32 = jnp.float32
BF16 = jnp.bfloat16
HI = lax.Precision.HIGHEST

D_MODEL = 2048
D_FF = 5632
NORM_EPS = 1e-6
NEG_INF = -1e30
FORCE = 1e6
LOG_FLOOR = 1e-30

HEADS = 8
DH = 128
CHUNK = 64
GDN_CONV = 4
NSA_GROUPS = 2
NSA_REP = HEADS // NSA_GROUPS
CMP_LEN = 32
CMP_STRIDE = 16
SLC_LEN = 64
SLC_SHIFT = 6
SLC_TOPN = 8
WIN_LEN = 512
NSA_TQ = 128

OFF_GQ, OFF_GK, OFF_GV, OFF_GG = 0, 1024, 2048, 3072
OFF_NQ, OFF_NKV, OFF_SM = 4096, 5120, 6656
OFF_HQ, OFF_HF, OFF_HI, OFF_HG = 7168, 8192, 9216, 10240
OFF_MG = 11264
N_U = 17408
SM_BETA, SM_A, SM_NGATE = 0, 8, 16

VMEM_LIMIT = 56 * 1024 * 1024


def _cparams(sem):
    return pltpu.CompilerParams(dimension_semantics=sem, vmem_limit_bytes=VMEM_LIMIT)


def _sigmoid(x):
    return 1.0 / (1.0 + jnp.exp(-x))


def _silu(x):
    return x * _sigmoid(x)


def _rms(x, g):
    return x * lax.rsqrt(jnp.mean(x * x, axis=-1, keepdims=True) + NORM_EPS) * g


def _mm(a, b):
    return jnp.dot(a.astype(BF16), b.astype(BF16), preferred_element_type=F32)


def _mm_nt(a, b, prec=None):
    if prec is None:
        a, b = a.astype(BF16), b.astype(BF16)
    return lax.dot_general(a, b, (((1,), (1,)), ((), ())), precision=prec, preferred_element_type=F32)


def _mm_tn(a, b, prec=None):
    if prec is None:
        a, b = a.astype(BF16), b.astype(BF16)
    return lax.dot_general(a, b, (((0,), (0,)), ((), ())), precision=prec, preferred_element_type=F32)


def _bmm(a, b, prec=None):
    if prec is None:
        a, b = a.astype(BF16), b.astype(BF16)
    return jnp.einsum('cmk,ckn->cmn', a, b, precision=prec, preferred_element_type=F32)


def _bmm_nt(a, b, prec=None):
    if prec is None:
        a, b = a.astype(BF16), b.astype(BF16)
    return jnp.einsum('cmk,cnk->cmn', a, b, precision=prec, preferred_element_type=F32)


def _ffn_kernel(x_ref, g_ref, wg_ref, wu_ref, wd_ref, fg_ref, o_ref, h_ref, *, final_norm):
    f = pl.program_id(1)

    @pl.when(f == 0)
    def _():
        x = x_ref[...]
        h_ref[...] = _rms(x, g_ref[...]).astype(BF16)
        o_ref[...] = x

    h = h_ref[...]
    a = jnp.dot(h, wg_ref[...], preferred_element_type=F32)
    b = jnp.dot(h, wu_ref[...], preferred_element_type=F32)
    act = (_silu(a) * b * 0.5).astype(BF16)
    o_ref[...] += jnp.dot(act, wd_ref[...], preferred_element_type=F32)

    if final_norm:
        @pl.when(f == pl.num_programs(1) - 1)
        def _():
            o_ref[...] = _rms(o_ref[...], fg_ref[...])


def _ffn(x, g, wg, wu, wd, fg, final_norm, tm=512, tf=512):
    T, D = x.shape
    F = wg.shape[1]
    return pl.pallas_call(
        functools.partial(_ffn_kernel, final_norm=final_norm),
        grid=(T // tm, F // tf),
        in_specs=[
            pl.BlockSpec((tm, D), lambda i, f: (i, 0)),
            pl.BlockSpec((1, D), lambda i, f: (0, 0)),
            pl.BlockSpec((D, tf), lambda i, f: (0, f)),
            pl.BlockSpec((D, tf), lambda i, f: (0, f)),
            pl.BlockSpec((tf, D), lambda i, f: (f, 0)),
            pl.BlockSpec((1, D), lambda i, f: (0, 0)),
        ],
        out_specs=pl.BlockSpec((tm, D), lambda i, f: (i, 0)),
        out_shape=jax.ShapeDtypeStruct((T, D), F32),
        scratch_shapes=[pltpu.VMEM((tm, D), BF16)],
        compiler_params=_cparams(("parallel", "arbitrary")),
        name="ffn",
    )(x, g, wg, wu, wd, fg)


def _inproj_kernel(x_ref, g_ref, w_ref, o_ref, h_ref):
    @pl.when(pl.program_id(1) == 0)
    def _():
        h_ref[...] = _rms(x_ref[...], g_ref[...]).astype(BF16)

    o_ref[...] = jnp.dot(h_ref[...], w_ref[...], preferred_element_type=F32)


def _inproj(x, g, w, tm=1024, tn=1024):
    T, D = x.shape
    N = w.shape[1]
    return pl.pallas_call(
        _inproj_kernel,
        grid=(T // tm, N // tn),
        in_specs=[
            pl.BlockSpec((tm, D), lambda i, n: (i, 0)),
            pl.BlockSpec((1, D), lambda i, n: (0, 0)),
            pl.BlockSpec((D, tn), lambda i, n: (0, n)),
        ],
        out_specs=pl.BlockSpec((tm, tn), lambda i, n: (i, n)),
        out_shape=jax.ShapeDtypeStruct((T, N), F32),
        scratch_shapes=[pltpu.VMEM((tm, D), BF16)],
        compiler_params=_cparams(("parallel", "arbitrary")),
        name="in_proj",
    )(x, g, w)


def _pick_col(sm, col):
    r = lax.broadcasted_iota(jnp.int32, (DH, DH), 0)
    onehot = (r == col).astype(F32)
    return jnp.dot(sm, onehot, precision=HI, preferred_element_type=F32)


def _gdn_kernel(q_ref, k_ref, v_ref, gate_ref, sm_ref, cq_ref, ck_ref, cv_ref, alog_ref, dtb_ref, ng_ref,
                o_ref, u_s, w_s, attn_s, qg_s, kg_s, eg_s, o_s):
    h = pl.program_id(1)
    S = q_ref.shape[0]
    n = S // CHUNK
    C = CHUNK
    row = lax.broadcasted_iota(jnp.int32, (S, DH), 0)

    def conv_silu(x, w):
        y = x * w[GDN_CONV - 1:GDN_CONV]
        for j in range(1, GDN_CONV):
            xs = jnp.where(row >= j, pltpu.roll(x, j, axis=0), 0.0)
            y = y + xs * w[GDN_CONV - 1 - j:GDN_CONV - j]
        return _silu(y)

    def l2n(x):
        return x * lax.rsqrt(jnp.sum(x * x, axis=-1, keepdims=True) + 1e-6)

    q = l2n(conv_silu(q_ref[...], cq_ref[...])) * (DH ** -0.5)
    k = l2n(conv_silu(k_ref[...], ck_ref[...]))
    v = conv_silu(v_ref[...], cv_ref[...])

    sm = sm_ref[...]
    beta = _sigmoid(_pick_col(sm, SM_BETA + h))
    apre = _pick_col(sm, SM_A + h) + dtb_ref[...]
    softplus = jnp.maximum(apre, 0.0) + jnp.log1p(jnp.exp(-jnp.abs(apre)))
    g = -jnp.exp(alog_ref[...]) * softplus

    q3, k3, v3 = (t.reshape(n, C, DH) for t in (q, k, v))
    beta3 = beta.reshape(n, C, DH)
    g3 = g.reshape(n, C, DH)

    ti = lax.broadcasted_iota(jnp.int32, (n, C, C), 1)
    si = lax.broadcasted_iota(jnp.int32, (n, C, C), 2)
    causal = ti >= si
    strict = ti > si
    tril = causal.astype(F32)
    eye = (ti == si).astype(F32)

    gc3 = _bmm(tril, g3, HI)
    lane3 = lax.broadcasted_iota(jnp.int32, (n, C, DH), 2)
    gc_row = _bmm_nt(jnp.ones((n, C, DH), F32), jnp.where(lane3 == 0, gc3, 0.0), HI)
    diff = gc3[:, :, :C] - gc_row
    decay = jnp.where(causal, jnp.exp(jnp.where(causal, diff, 0.0)), 0.0)

    kb3 = k3 * beta3
    m = jnp.where(strict, _bmm_nt(kb3, k3, HI) * decay, 0.0)
    x = eye - m
    p = _bmm(m, m, HI)
    for i in range(5):
        x = x + _bmm(x, p, HI)
        if i < 4:
            p = _bmm(p, p, HI)
    egc = jnp.exp(gc3)
    uw = _bmm(x, jnp.concatenate([v3 * beta3, kb3 * egc], axis=-1), HI)
    u_s[...] = uw[:, :, :DH]
    w_s[...] = uw[:, :, DH:]
    attn_s[...] = jnp.where(causal, _bmm_nt(q3, k3, HI) * decay, 0.0)
    g_last = gc3[:, C - 1:C, :]
    qg_s[...] = q3 * egc
    kg_s[...] = k3 * jnp.exp(g_last - gc3)
    eg_s[...] = jnp.exp(g_last)

    def body(c, state):
        v_new = u_s[c] - jnp.dot(w_s[c], state, precision=HI, preferred_element_type=F32)
        o = (jnp.dot(qg_s[c], state, precision=HI, preferred_element_type=F32)
             + jnp.dot(attn_s[c], v_new, precision=HI, preferred_element_type=F32))
        o_s[c] = o
        return state * eg_s[c] + _mm_tn(kg_s[c], v_new, HI)

    lax.fori_loop(0, n, body, jnp.zeros((DH, DH), F32))

    o = o_s[...].reshape(S, DH)
    o_ref[...] = (_rms(o, ng_ref[...]) * _silu(gate_ref[...])).astype(o_ref.dtype)


def _gdn(u3, conv_w, a_log_b, dt_bias_b, norm_g):
    B, S, _ = u3.shape
    n = S // CHUNK

    def col(off):
        return pl.BlockSpec((None, S, DH), lambda b, h: (b, 0, off // DH + h))

    def cw(off):
        return pl.BlockSpec((GDN_CONV, DH), lambda b, h: (0, off // DH + h))

    return pl.pallas_call(
        _gdn_kernel,
        grid=(B, HEADS),
        in_specs=[
            col(OFF_GQ), col(OFF_GK), col(OFF_GV), col(OFF_GG),
            pl.BlockSpec((None, S, DH), lambda b, h: (b, 0, OFF_SM // DH)),
            cw(0), cw(1024), cw(2048),
            pl.BlockSpec((None, 1, DH), lambda b, h: (h, 0, 0)),
            pl.BlockSpec((None, 1, DH), lambda b, h: (h, 0, 0)),
            pl.BlockSpec((1, DH), lambda b, h: (0, 0)),
        ],
        out_specs=pl.BlockSpec((None, S, DH), lambda b, h: (b, 0, h)),
        out_shape=jax.ShapeDtypeStruct((B, S, HEADS * DH), BF16),
        scratch_shapes=[
            pltpu.VMEM((n, CHUNK, DH), F32), pltpu.VMEM((n, CHUNK, DH), F32),
            pltpu.VMEM((n, CHUNK, CHUNK), F32),
            pltpu.VMEM((n, CHUNK, DH), F32), pltpu.VMEM((n, CHUNK, DH), F32),
            pltpu.VMEM((n, 1, DH), F32), pltpu.VMEM((n, CHUNK, DH), F32),
        ],
        compiler_params=_cparams(("parallel", "arbitrary")),
        name="gdn",
    )(u3, u3, u3, u3, u3, conv_w, conv_w, conv_w, a_log_b, dt_bias_b, norm_g)


def _hgrn_kernel(q_ref, f_ref, i_ref, lb_ref, o_ref, oi_s, qb_s, kd_s, eb_s):
    S = q_ref.shape[0]
    C = CHUNK
    n = S // C
    lb = lb_ref[...]
    q = _silu(q_ref[...])
    f = lb + (1.0 - lb) * _sigmoid(f_ref[...])
    logf = jnp.log(jnp.maximum(f, LOG_FLOOR))
    k = 1.0 - f
    q3, k3, lf3 = (t.reshape(n, C, DH) for t in (q, k, logf))
    v3 = i_ref[...].reshape(n, C, DH)

    ti = lax.broadcasted_iota(jnp.int32, (n, C, C), 1)
    si = lax.broadcasted_iota(jnp.int32, (n, C, C), 2)
    tril = (ti >= si).astype(F32)
    b3 = _bmm(tril, lf3, HI)
    row3 = lax.broadcasted_iota(jnp.int32, (n, C, DH), 1)

    def ref_rows(t, blk, j):
        t4 = t.reshape(n, C // blk, blk, DH)
        return jnp.broadcast_to(t4[:, :, j:j + 1, :], t4.shape).reshape(n, C, DH)

    a = jnp.zeros((n, C, C), F32)
    for blk in (64, 32, 16):
        half = blk // 2
        bref = ref_rows(b3, blk, half)
        upper = (row3 & (blk - 1)) >= half
        ql = q3 * jnp.exp(jnp.where(upper, b3 - bref, NEG_INF))
        kl = k3 * jnp.exp(jnp.where(upper, NEG_INF, bref - b3))
        al = _bmm_nt(ql, kl)
        a = a + (al if blk == C else jnp.where((ti & -blk) == (si & -blk), al, 0.0))
    for j in range(8):
        bref = ref_rows(b3, 8, j)
        kref = ref_rows(k3, 8, j)
        xj = q3 * jnp.exp(jnp.where((row3 & 7) >= j, b3 - bref, NEG_INF)) * kref
        rj = jnp.sum(xj, axis=-1, keepdims=True)
        a = a + jnp.where(si == (ti & -8) + j, rj, 0.0)

    oi_s[...] = _bmm(a, v3)
    b_last = b3[:, C - 1:C, :]
    qb_s[...] = q3 * jnp.exp(b3)
    kd_s[...] = k3 * jnp.exp(b_last - b3)
    eb_s[...] = jnp.exp(b_last)

    def body(c, st):
        o_ref[pl.ds(pl.multiple_of(c * C, C), C), :] = oi_s[c] + _mm_nt(qb_s[c], st)
        return st * eb_s[c] + _mm_tn(i_ref[pl.ds(pl.multiple_of(c * C, C), C), :], kd_s[c])

    lax.fori_loop(0, n, body, jnp.zeros((DH, DH), F32))


def _hgrn(u3, lb_b):
    B, S, _ = u3.shape
    n = S // CHUNK

    def col(off):
        return pl.BlockSpec((None, S, DH), lambda b, h: (b, 0, off // DH + h))

    return pl.pallas_call(
        _hgrn_kernel,
        grid=(B, HEADS),
        in_specs=[col(OFF_HQ), col(OFF_HF), col(OFF_HI),
                  pl.BlockSpec((None, 1, DH), lambda b, h: (h, 0, 0))],
        out_specs=pl.BlockSpec((None, S, DH), lambda b, h: (b, 0, h)),
        out_shape=jax.ShapeDtypeStruct((B, S, HEADS * DH), F32),
        scratch_shapes=[
            pltpu.VMEM((n, CHUNK, DH), F32), pltpu.VMEM((n, CHUNK, DH), F32),
            pltpu.VMEM((n, CHUNK, DH), F32), pltpu.VMEM((n, 1, DH), F32),
        ],
        compiler_params=_cparams(("parallel", "arbitrary")),
        name="hgrn2",
    )(u3, u3, u3, lb_b)


def _masked_softmax(s, ok):
    m = jnp.max(jnp.where(ok, s, NEG_INF), axis=-1, keepdims=True)
    e = jnp.where(ok, jnp.exp(s - m), 0.0)
    den = jnp.sum(e, axis=-1, keepdims=True)
    return e / jnp.where(den > 0.0, den, 1.0)


def _nsa_kernel(q_ref, kc16_ref, vc16_ref, ks_ref, vs_ref, kw_ref, vw_ref, sm_ref,
                pek_ref, pev_ref, wk1_ref, wk2_ref, wv1_ref, wv2_ref,
                o_ref, kc_s, vc_s, ks_s, vs_s, kw_s, vw_s):
    g = pl.program_id(1)
    qi = pl.program_id(2)
    S = ks_ref.shape[0]
    TQ = NSA_TQ
    R = NSA_REP
    NC = S // CMP_STRIDE
    half = CMP_STRIDE * DH

    @pl.when(qi == 0)
    def _():
        def compress(x16_ref, pe_ref, w1_ref, w2_ref):
            x16 = x16_ref[...]
            pe = pe_ref[...]
            a = jnp.dot(x16 + pe[0:1], w1_ref[0:half, :], precision=HI, preferred_element_type=F32)
            b = jnp.dot(x16 + pe[1:2], w1_ref[half:, :], precision=HI, preferred_element_type=F32)
            pre = a + pltpu.roll(b, NC - 1, axis=0)
            return jnp.dot(_silu(pre), w2_ref[...], precision=HI, preferred_element_type=F32)

        kc_s[...] = compress(kc16_ref, pek_ref, wk1_ref, wk2_ref)
        vc_s[...] = compress(vc16_ref, pev_ref, wv1_ref, wv2_ref)
        ks_s[...] = ks_ref[...].astype(BF16)
        vs_s[...] = vs_ref[...].astype(BF16)
        kw_s[0:WIN_LEN, :] = jnp.zeros((WIN_LEN, DH), BF16)
        vw_s[0:WIN_LEN, :] = jnp.zeros((WIN_LEN, DH), BF16)
        kw_s[WIN_LEN:, :] = kw_ref[...].astype(BF16)
        vw_s[WIN_LEN:, :] = vw_ref[...].astype(BF16)

    t0 = qi * TQ
    qb = q_ref[...] * (DH ** -0.5)
    q4 = jnp.concatenate([qb[:, r * DH:(r + 1) * DH] for r in range(R)], axis=0)
    q4b = q4.astype(BF16)

    def tok(shape):
        return t0 + (lax.broadcasted_iota(jnp.int32, shape, 0) & (TQ - 1))

    def lane(shape):
        return lax.broadcasted_iota(jnp.int32, shape, 1)

    sc = _mm_nt(q4, kc_s[...], HI)
    n_id = lane((R * TQ, NC))
    ok_c = (n_id * CMP_STRIDE + (CMP_LEN - 1) <= tok((R * TQ, NC))) & (n_id < NC - 1)
    p_c = _masked_softmax(sc, ok_c)
    o_cmp = jnp.dot(p_c, vc_s[...], precision=HI, preferred_element_type=F32)

    p_sum = p_c[0:TQ]
    for r in range(1, R):
        p_sum = p_sum + p_c[r * TQ:(r + 1) * TQ]
    cn = lax.broadcasted_iota(jnp.int32, (NC, DH), 0) * CMP_STRIDE
    js = lax.broadcasted_iota(jnp.int32, (NC, DH), 1) * SLC_LEN
    overlap = ((cn < js + SLC_LEN) & (cn + CMP_LEN > js)).astype(F32)
    imp = jnp.dot(p_sum, overlap, precision=HI, preferred_element_type=F32)
    blk = lane((TQ, DH))
    cur = tok((TQ, DH)) >> SLC_SHIFT
    forced = (blk == 0) | (blk == cur) | (blk == cur - 1)
    imp = jnp.where(blk > cur, -FORCE, jnp.where(forced, FORCE, imp))
    work = jnp.where(blk < S // SLC_LEN, imp, -jnp.inf)
    blkf = blk.astype(F32)
    sel = jnp.zeros((TQ, DH), F32)
    for _ in range(SLC_TOPN):
        mx = jnp.max(work, axis=-1, keepdims=True)
        idx = jnp.min(jnp.where(work == mx, blkf, float(DH)), axis=-1, keepdims=True)
        hit = blkf == idx
        sel = jnp.where(hit, 1.0, sel)
        work = jnp.where(hit, -jnp.inf, work)

    expand = ((lax.broadcasted_iota(jnp.int32, (DH, S), 1) >> SLC_SHIFT)
              == lax.broadcasted_iota(jnp.int32, (DH, S), 0))
    sel_keys = jnp.dot(sel.astype(BF16), jnp.where(expand, 1.0, 0.0).astype(BF16), preferred_element_type=F32)
    sel_keys = jnp.where(lane((TQ, S)) <= tok((TQ, S)), sel_keys, 0.0)
    ok_s = jnp.concatenate([sel_keys] * R, axis=0) > 0.5
    p_s = _masked_softmax(_mm_nt(q4b, ks_s[...]), ok_s)
    o_slc = jnp.dot(p_s.astype(BF16), vs_s[...], preferred_element_type=F32)

    span = WIN_LEN + TQ
    start = pl.multiple_of(t0, TQ)
    s_w = _mm_nt(q4b, kw_s[pl.ds(start, span), :])
    pos = t0 - WIN_LEN + lane((R * TQ, span))
    t_w = tok((R * TQ, span))
    ok_w = (pos <= t_w) & (pos > t_w - WIN_LEN) & (pos >= 0)
    p_w = _masked_softmax(s_w, ok_w)
    o_win = jnp.dot(p_w.astype(BF16), vw_s[pl.ds(start, span), :], preferred_element_type=F32)

    sg = _sigmoid(sm_ref[...])
    sl = lane((TQ, DH))
    for r in range(R):
        base = SM_NGATE + (g * R + r) * 3
        gate = [jnp.sum(jnp.where(sl == base + i, sg, 0.0), axis=-1, keepdims=True) for i in range(3)]
        rows = slice(r * TQ, (r + 1) * TQ)
        o_ref[:, r * DH:(r + 1) * DH] = (gate[0] * o_cmp[rows] + gate[1] * o_slc[rows]
                                         + gate[2] * o_win[rows]).astype(o_ref.dtype)


def _nsa(u3, kc16, vc16, pe_k, pe_v, wk1, wk2, wv1, wv2):
    B, S, _ = u3.shape
    G, R, TQ = NSA_GROUPS, NSA_REP, NSA_TQ
    NC = S // CMP_STRIDE

    def kv(i):
        return pl.BlockSpec((None, S, DH), lambda b, g, t: (b, 0, OFF_NKV // DH + i * G + g))

    def full(shape):
        return pl.BlockSpec(shape, lambda b, g, t: (0,) * len(shape))

    c16 = pl.BlockSpec((None, None, NC, CMP_STRIDE * DH), lambda b, g, t: (b, g, 0, 0))
    return pl.pallas_call(
        _nsa_kernel,
        grid=(B, G, S // TQ),
        in_specs=[
            pl.BlockSpec((None, TQ, R * DH), lambda b, g, t: (b, t, OFF_NQ // (R * DH) + g)),
            c16, c16, kv(2), kv(3), kv(4), kv(5),
            pl.BlockSpec((None, TQ, DH), lambda b, g, t: (b, t, OFF_SM // DH)),
            full((2, CMP_STRIDE * DH)), full((2, CMP_STRIDE * DH)),
            full((CMP_LEN * DH, DH)), full((DH, DH)), full((CMP_LEN * DH, DH)), full((DH, DH)),
        ],
        out_specs=pl.BlockSpec((None, TQ, R * DH), lambda b, g, t: (b, t, g)),
        out_shape=jax.ShapeDtypeStruct((B, S, HEADS * DH), BF16),
        scratch_shapes=[
            pltpu.VMEM((NC, DH), F32), pltpu.VMEM((NC, DH), F32),
            pltpu.VMEM((S, DH), BF16), pltpu.VMEM((S, DH), BF16),
            pltpu.VMEM((S + WIN_LEN, DH), BF16), pltpu.VMEM((S + WIN_LEN, DH), BF16),
        ],
        compiler_params=_cparams(("parallel", "parallel", "arbitrary")),
        name="nsa",
    )(u3, kc16, vc16, u3, u3, u3, u3, u3, pe_k, pe_v, wk1, wk2, wv1, wv2)


def _merge_kernel(x_ref, ya_ref, yb_ref, oc_ref, hg_ref, hn_ref, ga_ref, gb_ref, gc_ref,
                  pa_ref, pb_ref, pc_ref, wo_ref, o_ref, yc_s):
    @pl.when(pl.program_id(1) == 0)
    def _():
        yc_s[...] = (_rms(oc_ref[...], hn_ref[...]) * _silu(hg_ref[...])).astype(BF16)
        o_ref[...] = x_ref[...]

    m = (_sigmoid(ga_ref[...]) * jnp.dot(ya_ref[...], pa_ref[...], preferred_element_type=F32)
         + _sigmoid(gb_ref[...]) * jnp.dot(yb_ref[...], pb_ref[...], preferred_element_type=F32)
         + _sigmoid(gc_ref[...]) * jnp.dot(yc_s[...], pc_ref[...], preferred_element_type=F32))
    o_ref[...] += jnp.dot(m.astype(BF16), wo_ref[...], preferred_element_type=F32)


def _merge(x, u, ya, yb, oc, hnorm, pa, pb, pc, wo, tm=512, tk=512):
    T, D = x.shape
    V = ya.shape[1]

    def gate(i):
        return pl.BlockSpec((tm, tk), lambda m, c: (m, (OFF_MG + i * D) // tk + c))

    rows = lambda w: pl.BlockSpec((tm, w), lambda m, c: (m, 0))
    proj = pl.BlockSpec((V, tk), lambda m, c: (0, c))
    return pl.pallas_call(
        _merge_kernel,
        grid=(T // tm, D // tk),
        in_specs=[
            rows(D), rows(V), rows(V), rows(V),
            pl.BlockSpec((tm, V), lambda m, c: (m, OFF_HG // V)),
            pl.BlockSpec((1, V), lambda m, c: (0, 0)),
            gate(0), gate(1), gate(2),
            proj, proj, proj,
            pl.BlockSpec((tk, D), lambda m, c: (c, 0)),
        ],
        out_specs=rows(D),
        out_shape=jax.ShapeDtypeStruct((T, D), F32),
        scratch_shapes=[pltpu.VMEM((tm, V), BF16)],
        compiler_params=_cparams(("parallel", "arbitrary")),
        name="merge",
    )(x, ya, yb, oc, u, hnorm, u, u, u, pa, pb, pc, wo)


def _relayout_w_in(w):
    D = w.shape[0]
    o = 0
    qkv = w[:, o:o + 3072]; o += 3072
    beta = w[:, o:o + 8]; o += 8
    a = w[:, o:o + 8]; o += 8
    ggate = w[:, o:o + 1024]; o += 1024
    nq = w[:, o:o + 1024]; o += 1024
    nkv = w[:, o:o + 1536]; o += 1536
    ngate = w[:, o:o + 24]; o += 24
    hg = w[:, o:o + 4096]; o += 4096
    mg = w[:, o:o + 6144]; o += 6144
    small = jnp.concatenate([beta, a, ngate, jnp.zeros((D, OFF_HQ - OFF_SM - 40), w.dtype)], axis=1)
    return jnp.concatenate([qkv, ggate, nq, nkv, small, hg, mg], axis=1)


def kernel(x, ffn1_norm, ffn1_w_gate, ffn1_w_up, ffn1_w_down, mix_norm, w_in, gdn_conv, gdn_a_log, gdn_dt_bias, gdn_out_norm, nsa_cmp_pe_k, nsa_cmp_pe_v, nsa_cmp_k_w1, nsa_cmp_k_w2, nsa_cmp_v_w1, nsa_cmp_v_w2, hgrn_lb_logits, hgrn_out_norm, w_proj_a, w_proj_b, w_proj_c, w_out, ffn2_norm, ffn2_w_gate, ffn2_w_up, ffn2_w_down, final_norm):
    B, S, D = x.shape
    L = w_in.shape[0]
    T = B * S
    G = NSA_GROUPS
    row = lambda v: v.reshape(1, -1).astype(F32)
    bf = lambda w: w.astype(BF16)
    rep = lambda v: jnp.broadcast_to(v.astype(F32).reshape(HEADS, 1, -1), (HEADS, 1, DH))

    lb_p = jax.nn.softmax(hgrn_lb_logits.astype(F32), axis=0)
    lower_bounds = jnp.cumsum(lb_p, axis=0) - lb_p[0]
    fin = row(final_norm)

    xf = x.reshape(T, D)
    for l in range(L):
        xf = _ffn(xf, row(ffn1_norm[l]), bf(ffn1_w_gate[l]), bf(ffn1_w_up[l]), bf(ffn1_w_down[l]), fin, False)

        u = _inproj(xf, row(mix_norm[l]), bf(_relayout_w_in(w_in[l])))
        u3 = u.reshape(B, S, N_U)
        ya = _gdn(u3, gdn_conv[l].astype(F32), rep(gdn_a_log[l]), rep(gdn_dt_bias[l]), row(gdn_out_norm[l]))
        kv = u3[:, :, OFF_NKV:OFF_NKV + 2 * G * DH].reshape(B, S // CMP_STRIDE, CMP_STRIDE, 2, G, DH)
        kv16 = kv.transpose(3, 0, 4, 1, 2, 5).reshape(2, B, G, S // CMP_STRIDE, CMP_STRIDE * DH)
        pe2 = lambda pe: pe.astype(F32).reshape(2, CMP_STRIDE * DH)
        yb = _nsa(u3, kv16[0], kv16[1], pe2(nsa_cmp_pe_k[l]), pe2(nsa_cmp_pe_v[l]),
                  nsa_cmp_k_w1[l], nsa_cmp_k_w2[l], nsa_cmp_v_w1[l], nsa_cmp_v_w2[l])
        oc = _hgrn(u3, lower_bounds[l].reshape(HEADS, 1, DH))
        xf = _merge(xf, u, ya.reshape(T, -1), yb.reshape(T, -1), oc.reshape(T, -1), row(hgrn_out_norm[l]),
                    bf(w_proj_a[l]), bf(w_proj_b[l]), bf(w_proj_c[l]), bf(w_out[l]))

        xf = _ffn(xf, row(ffn2_norm[l]), bf(ffn2_w_gate[l]), bf(ffn2_w_up[l]), bf(ffn2_w_down[l]), fin,
                  l == L - 1)
    return xf.reshape(B, S, D)
```

```python
import functools

import jax
import jax.numpy as jnp
from jax import lax
from jax.experimental import pallas as pl
from jax.experimental.pallas import tpu as pltpu

F32 = jnp.float32
BF16 = jnp.bfloat16
HI = lax.Precision.HIGHEST

D_MODEL = 2048
D_FF = 5632
NORM_EPS = 1e-6
NEG_INF = -1e30
FORCE = 1e6
LOG_FLOOR = 1e-30

HEADS = 8
DH = 128
CHUNK = 64
GDN_CONV = 4
NSA_GROUPS = 2
NSA_REP = HEADS // NSA_GROUPS
CMP_LEN = 32
CMP_STRIDE = 16
SLC_LEN = 64
SLC_SHIFT = 6
SLC_TOPN = 8
WIN_LEN = 512
NSA_TQ = 128
NSA_KEY_STEP = 512

OFF_GQ, OFF_GK, OFF_GV, OFF_GG = 0, 1024, 2048, 3072
OFF_NQ, OFF_NKV, OFF_SM = 4096, 5120, 6656
OFF_HQ, OFF_HF, OFF_HI, OFF_HG = 7168, 8192, 9216, 10240
OFF_MG = 11264
N_U = 17408
SM_BETA, SM_A, SM_NGATE = 0, 8, 16

VMEM_LIMIT = 56 * 1024 * 1024


def _cparams(sem):
    return pltpu.CompilerParams(dimension_semantics=sem, vmem_limit_bytes=VMEM_LIMIT)


def _sigmoid(x):
    return 1.0 / (1.0 + jnp.exp(-x))


def _silu(x):
    return x * _sigmoid(x)


def _rms(x, g):
    return x * lax.rsqrt(jnp.mean(x * x, axis=-1, keepdims=True) + NORM_EPS) * g


def _mm_nt(a, b, prec=None):
    if prec is None:
        a, b = a.astype(BF16), b.astype(BF16)
    return lax.dot_general(a, b, (((1,), (1,)), ((), ())), precision=prec, preferred_element_type=F32)


def _mm_tn(a, b, prec=None):
    if prec is None:
        a, b = a.astype(BF16), b.astype(BF16)
    return lax.dot_general(a, b, (((0,), (0,)), ((), ())), precision=prec, preferred_element_type=F32)


def _bmm(a, b):
    return jnp.einsum('cmk,ckn->cmn', a.astype(BF16), b.astype(BF16), preferred_element_type=F32)


def _bmm_nt(a, b):
    return jnp.einsum('cmk,cnk->cmn', a.astype(BF16), b.astype(BF16), preferred_element_type=F32)


def _bmm_tn(a, b):
    return jnp.einsum('ckm,ckn->cmn', a.astype(BF16), b.astype(BF16), preferred_element_type=F32)


def _split3(x):
    hi = x.astype(BF16)
    r = x - hi.astype(F32)
    mid = r.astype(BF16)
    lo = (r - mid.astype(F32)).astype(BF16)
    return hi, mid, lo


def _ffn_kernel(x_ref, g_ref, wg_ref, wu_ref, wd_ref, fg_ref, o_ref, h_ref, *, final_norm):
    f = pl.program_id(1)

    @pl.when(f == 0)
    def _():
        x = x_ref[...]
        h_ref[...] = _rms(x, g_ref[...]).astype(BF16)
        o_ref[...] = x

    h = h_ref[...]
    a = jnp.dot(h, wg_ref[...], preferred_element_type=F32)
    b = jnp.dot(h, wu_ref[...], preferred_element_type=F32)
    act = (_silu(a) * b * 0.5).astype(BF16)
    o_ref[...] += jnp.dot(act, wd_ref[...], preferred_element_type=F32)

    if final_norm:
        @pl.when(f == pl.num_programs(1) - 1)
        def _():
            o_ref[...] = _rms(o_ref[...], fg_ref[...])


def _ffn(x, g, wg, wu, wd, fg, final_norm, tm=512, tf=512):
    T, D = x.shape
    F = wg.shape[1]
    return pl.pallas_call(
        functools.partial(_ffn_kernel, final_norm=final_norm),
        grid=(T // tm, F // tf),
        in_specs=[
            pl.BlockSpec((tm, D), lambda i, f: (i, 0)),
            pl.BlockSpec((1, D), lambda i, f: (0, 0)),
            pl.BlockSpec((D, tf), lambda i, f: (0, f)),
            pl.BlockSpec((D, tf), lambda i, f: (0, f)),
            pl.BlockSpec((tf, D), lambda i, f: (f, 0)),
            pl.BlockSpec((1, D), lambda i, f: (0, 0)),
        ],
        out_specs=pl.BlockSpec((tm, D), lambda i, f: (i, 0)),
        out_shape=jax.ShapeDtypeStruct((T, D), F32),
        scratch_shapes=[pltpu.VMEM((tm, D), BF16)],
        compiler_params=_cparams(("parallel", "arbitrary")),
        name="ffn",
    )(x, g, wg, wu, wd, fg)


def _inproj_kernel(x_ref, g_ref, w_ref, o_ref, h_ref):
    @pl.when(pl.program_id(1) == 0)
    def _():
        h_ref[...] = _rms(x_ref[...], g_ref[...]).astype(BF16)

    o_ref[...] = jnp.dot(h_ref[...], w_ref[...], preferred_element_type=F32)


def _inproj(x, g, w, tm=1024, tn=1024):
    T, D = x.shape
    N = w.shape[1]
    return pl.pallas_call(
        _inproj_kernel,
        grid=(T // tm, N // tn),
        in_specs=[
            pl.BlockSpec((tm, D), lambda i, n: (i, 0)),
            pl.BlockSpec((1, D), lambda i, n: (0, 0)),
            pl.BlockSpec((D, tn), lambda i, n: (0, n)),
        ],
        out_specs=pl.BlockSpec((tm, tn), lambda i, n: (i, n)),
        out_shape=jax.ShapeDtypeStruct((T, N), F32),
        scratch_shapes=[pltpu.VMEM((tm, D), BF16)],
        compiler_params=_cparams(("parallel", "arbitrary")),
        name="in_proj",
    )(x, g, w)


def _gdn_kernel(q_ref, k_ref, v_ref, gate_ref, sm_ref, cq_ref, ck_ref, cv_ref, alog_ref, dtb_ref, ng_ref,
                o_ref, kwt_s, bt_s, sp_s, eg_s):
    h = pl.program_id(1)
    S = q_ref.shape[0]
    n = S // CHUNK
    C = CHUNK
    row = lax.broadcasted_iota(jnp.int32, (S, DH), 0)

    def conv_silu(x, w):
        y = x * w[GDN_CONV - 1:GDN_CONV]
        for j in range(1, GDN_CONV):
            xs = jnp.where(row >= j, pltpu.roll(x, j, axis=0), 0.0)
            y = y + xs * w[GDN_CONV - 1 - j:GDN_CONV - j]
        return _silu(y)

    def l2n(x):
        return x * lax.rsqrt(jnp.sum(x * x, axis=-1, keepdims=True) + 1e-6)

    q = l2n(conv_silu(q_ref[...], cq_ref[...])) * (DH ** -0.5)
    k = l2n(conv_silu(k_ref[...], ck_ref[...]))
    v = conv_silu(v_ref[...], cv_ref[...])

    r2 = lax.broadcasted_iota(jnp.int32, (DH, 2 * DH), 0)
    c2 = lax.broadcasted_iota(jnp.int32, (DH, 2 * DH), 1)
    onehot = jnp.where(r2 == jnp.where(c2 < DH, SM_BETA + h, SM_A + h), 1.0, 0.0).astype(BF16)
    ba = sum(jnp.dot(p, onehot, preferred_element_type=F32) for p in _split3(sm_ref[...]))
    beta = _sigmoid(ba[:, :DH])
    apre = ba[:, DH:] + dtb_ref[...]
    softplus = jnp.maximum(apre, 0.0) + jnp.log1p(jnp.exp(-jnp.abs(apre)))
    g = -jnp.exp(alog_ref[...]) * softplus

    q3, k3, v3 = (t.reshape(n, C, DH) for t in (q, k, v))
    beta3 = beta.reshape(n, C, DH)
    g3 = g.reshape(n, C, DH)

    ti = lax.broadcasted_iota(jnp.int32, (n, C, C), 1)
    si = lax.broadcasted_iota(jnp.int32, (n, C, C), 2)
    causal = ti >= si
    strict = ti > si
    tril = jnp.where(causal, 1.0, 0.0).astype(BF16)
    eye = jnp.where(ti == si, 1.0, 0.0)

    gc3 = sum(_bmm(tril, p) for p in _split3(g3))
    lane3 = lax.broadcasted_iota(jnp.int32, (n, C, DH), 2)
    ones = jnp.ones((n, C, DH), BF16)
    gc_row = sum(_bmm_nt(ones, p) for p in _split3(jnp.where(lane3 == 0, gc3, 0.0)))
    decay = jnp.where(causal, jnp.exp(jnp.where(causal, gc3[:, :, :C] - gc_row, 0.0)), 0.0)

    kb3 = k3 * beta3
    m = jnp.where(strict, _bmm_nt(kb3, k3) * decay, 0.0)
    def lower_left(blk):
        half = blk // 2
        pair = ((ti & (blk - 1)) >= half) & ((si & (blk - 1)) < half)
        if blk < C:
            pair = pair & ((ti & -blk) == (si & -blk))
        return jnp.where(pair, m, 0.0)

    x = eye - lower_left(2)
    for blk in (4, 8, 16, 32, 64):
        x = x - _bmm(_bmm(x, lower_left(blk)), x)
    egc = jnp.exp(gc3)
    uw = _bmm(x, jnp.concatenate([v3 * beta3, kb3 * egc], axis=-1))
    attn = jnp.where(causal, _bmm_nt(q3, k3) * decay, 0.0)
    g_last = gc3[:, C - 1:C, :]
    kg = k3 * jnp.exp(g_last - gc3)
    au = _bmm(attn, uw)
    o0 = au[:, :, :DH]
    qeff = q3 * egc - au[:, :, DH:]
    kwb = _bmm_tn(uw, kg)
    bt_s[...] = kwb[:, :DH, :]
    kwt_s[...] = kwb[:, DH:, :].astype(BF16)
    eg_s[...] = jnp.exp(g_last)

    def body(c, st):
        stb = st.astype(BF16)
        sp_s[c] = stb
        return st * eg_s[c] - jnp.dot(stb, kwt_s[c], preferred_element_type=F32) + bt_s[c]

    lax.fori_loop(0, n, body, jnp.zeros((DH, DH), F32))

    o = (o0 + _bmm_nt(qeff, sp_s[...])).reshape(S, DH)
    o_ref[...] = (_rms(o, ng_ref[...]) * _silu(gate_ref[...])).astype(o_ref.dtype)


def _gdn(u3, conv_w, a_log_b, dt_bias_b, norm_g):
    B, S, _ = u3.shape
    n = S // CHUNK

    def col(off):
        return pl.BlockSpec((None, S, DH), lambda b, h: (b, 0, off // DH + h))

    def cw(off):
        return pl.BlockSpec((GDN_CONV, DH), lambda b, h: (0, off // DH + h))

    return pl.pallas_call(
        _gdn_kernel,
        grid=(B, HEADS),
        in_specs=[
            col(OFF_GQ), col(OFF_GK), col(OFF_GV), col(OFF_GG),
            pl.BlockSpec((None, S, DH), lambda b, h: (b, 0, OFF_SM // DH)),
            cw(0), cw(1024), cw(2048),
            pl.BlockSpec((None, 1, DH), lambda b, h: (h, 0, 0)),
            pl.BlockSpec((None, 1, DH), lambda b, h: (h, 0, 0)),
            pl.BlockSpec((1, DH), lambda b, h: (0, 0)),
        ],
        out_specs=pl.BlockSpec((None, S, DH), lambda b, h: (b, 0, h)),
        out_shape=jax.ShapeDtypeStruct((B, S, HEADS * DH), BF16),
        scratch_shapes=[
            pltpu.VMEM((n, DH, DH), BF16), pltpu.VMEM((n, DH, DH), F32),
            pltpu.VMEM((n, DH, DH), BF16), pltpu.VMEM((n, 1, DH), F32),
        ],
        compiler_params=_cparams(("parallel", "arbitrary")),
        name="gdn",
    )(u3, u3, u3, u3, u3, conv_w, conv_w, conv_w, a_log_b, dt_bias_b, norm_g)


def _hgrn_kernel(q_ref, f_ref, i_ref, lb_ref, o_ref, kv_s, sp_s, eb_s):
    S = q_ref.shape[0]
    C = CHUNK
    n = S // C
    lb = lb_ref[...]
    q = _silu(q_ref[...])
    f = lb + (1.0 - lb) * _sigmoid(f_ref[...])
    logf = jnp.log(jnp.maximum(f, LOG_FLOOR))
    k = 1.0 - f
    v3 = i_ref[...].reshape(n, C, DH)

    ti = lax.broadcasted_iota(jnp.int32, (n, C, C), 1)
    si = lax.broadcasted_iota(jnp.int32, (n, C, C), 2)
    tril = jnp.where(ti >= si, 1.0, 0.0).astype(BF16)
    b3 = sum(_bmm(tril, p) for p in _split3(logf.reshape(n, C, DH)))
    b = b3.reshape(S, DH)
    row = lax.broadcasted_iota(jnp.int32, (S, DH), 0)

    def mid_rows(blk):
        half = blk // 2
        if blk >= 8:
            t4 = b.reshape(S // blk, blk, DH)
            return jnp.broadcast_to(t4[:, half:half + 1, :], t4.shape).reshape(S, DH)
        if blk == 4:
            d = jnp.where((row & 1) == 1, pltpu.roll(b, 1, axis=0), b)
            return jnp.where((row & 3) < 2, pltpu.roll(d, S - 2, axis=0), d)
        return jnp.where((row & 1) == 0, pltpu.roll(b, S - 1, axis=0), b)

    a = jnp.where(ti == si, jnp.sum((q * k).reshape(n, C, DH), axis=-1, keepdims=True), 0.0)
    for blk in (64, 32, 16, 8, 4, 2):
        half = blk // 2
        upper = (row & (blk - 1)) >= half
        x = (jnp.where(upper, q, k) * jnp.exp(-jnp.abs(b - mid_rows(blk)))).reshape(n, C, DH)
        pair = ((ti & (blk - 1)) >= half) & ((si & (blk - 1)) < half)
        if blk < C:
            pair = pair & ((ti & -blk) == (si & -blk))
        a = jnp.where(pair, _bmm_nt(x, x), a)

    o_intra = _bmm(a, v3)
    b_last = b3[:, C - 1:C, :]
    q3 = q.reshape(n, C, DH)
    kv_s[...] = _bmm_tn(v3, k.reshape(n, C, DH) * jnp.exp(b_last - b3))
    eb_s[...] = jnp.exp(b_last)

    def body(c, st):
        sp_s[c] = st.astype(BF16)
        return st * eb_s[c] + kv_s[c]

    lax.fori_loop(0, n, body, jnp.zeros((DH, DH), F32))
    o_ref[...] = (o_intra + _bmm_nt(q3 * jnp.exp(b3), sp_s[...])).reshape(S, DH)


def _hgrn(u3, lb_b):
    B, S, _ = u3.shape
    n = S // CHUNK

    def col(off):
        return pl.BlockSpec((None, S, DH), lambda b, h: (b, 0, off // DH + h))

    return pl.pallas_call(
        _hgrn_kernel,
        grid=(B, HEADS),
        in_specs=[col(OFF_HQ), col(OFF_HF), col(OFF_HI),
                  pl.BlockSpec((None, 1, DH), lambda b, h: (h, 0, 0))],
        out_specs=pl.BlockSpec((None, S, DH), lambda b, h: (b, 0, h)),
        out_shape=jax.ShapeDtypeStruct((B, S, HEADS * DH), F32),
        scratch_shapes=[
            pltpu.VMEM((n, DH, DH), F32), pltpu.VMEM((n, DH, DH), BF16), pltpu.VMEM((n, 1, DH), F32),
        ],
        compiler_params=_cparams(("parallel", "arbitrary")),
        name="hgrn2",
    )(u3, u3, u3, lb_b)


def _nsa_kernel(q_ref, kc16_ref, vc16_ref, ks_ref, vs_ref, kw_ref, vw_ref, sm_ref,
                pek_ref, pev_ref, wk1_ref, wk2_ref, wv1_ref, wv2_ref,
                o_ref, kc_s, vc_s, ks_s, vs_s, kw_s, vw_s, oslc_s):
    g = pl.program_id(1)
    qi = pl.program_id(2)
    S = ks_ref.shape[0]
    TQ = NSA_TQ
    R = NSA_REP
    NC = S // CMP_STRIDE
    NB = S // SLC_LEN
    half = CMP_STRIDE * DH

    @pl.when(qi == 0)
    def _():
        def compress(x16_ref, pe_ref, w1_ref, w2_ref):
            x16 = x16_ref[...]
            pe = pe_ref[...]
            a = jnp.dot(x16 + pe[0:1], w1_ref[0:half, :], precision=HI, preferred_element_type=F32)
            b = jnp.dot(x16 + pe[1:2], w1_ref[half:, :], precision=HI, preferred_element_type=F32)
            pre = a + pltpu.roll(b, NC - 1, axis=0)
            return jnp.dot(_silu(pre), w2_ref[...], precision=HI, preferred_element_type=F32)

        kc_s[...] = compress(kc16_ref, pek_ref, wk1_ref, wk2_ref)
        vc_s[...] = compress(vc16_ref, pev_ref, wv1_ref, wv2_ref)
        ks_s[...] = ks_ref[...].astype(BF16)
        vs_s[...] = vs_ref[...].astype(BF16)
        kw_s[0:WIN_LEN, :] = jnp.zeros((WIN_LEN, DH), BF16)
        vw_s[0:WIN_LEN, :] = jnp.zeros((WIN_LEN, DH), BF16)
        kw_s[WIN_LEN:, :] = kw_ref[...].astype(BF16)
        vw_s[WIN_LEN:, :] = vw_ref[...].astype(BF16)

    t0 = qi * TQ
    qb = q_ref[...] * (DH ** -0.5)
    q4 = jnp.concatenate([qb[:, r * DH:(r + 1) * DH] for r in range(R)], axis=0)
    q4b = q4.astype(BF16)

    def iota(shape, dim):
        return lax.broadcasted_iota(jnp.int32, shape, dim)

    sc = _mm_nt(q4, kc_s[...], HI)
    n_id = iota((R * TQ, NC), 1)
    t_c = t0 + (iota((R * TQ, NC), 0) & (TQ - 1))
    ok_c = (n_id * CMP_STRIDE + (CMP_LEN - 1) <= t_c) & (n_id < NC - 1)
    m_c = jnp.max(jnp.where(ok_c, sc, NEG_INF), axis=-1, keepdims=True)
    e_c = jnp.where(ok_c, jnp.exp(sc - m_c), 0.0)
    den_c = jnp.sum(e_c, axis=-1, keepdims=True)
    p_c = e_c / jnp.where(den_c > 0.0, den_c, 1.0)
    o_cmp = jnp.dot(p_c, vc_s[...], precision=HI, preferred_element_type=F32)

    p_sum = p_c[0:TQ]
    for r in range(1, R):
        p_sum = p_sum + p_c[r * TQ:(r + 1) * TQ]
    js = iota((NB, NC), 0) * SLC_LEN
    cn = iota((NB, NC), 1) * CMP_STRIDE
    overlap_t = jnp.where((cn < js + SLC_LEN) & (cn + CMP_LEN > js), 1.0, 0.0)
    imp = _mm_nt(overlap_t, p_sum, HI)
    blk = iota((NB, TQ), 0)
    cur = (t0 + iota((NB, TQ), 1)) >> SLC_SHIFT
    forced = (blk == 0) | (blk == cur) | (blk == cur - 1)
    work = jnp.where(blk > cur, -FORCE, jnp.where(forced, FORCE, imp))
    blkf = blk.astype(F32)
    sel_bias = jnp.full((NB, TQ), NEG_INF, F32)
    for _ in range(SLC_TOPN):
        mx = jnp.max(work, axis=0, keepdims=True)
        idx = jnp.min(jnp.where(work == mx, blkf, float(NB)), axis=0, keepdims=True)
        hit = blkf == idx
        sel_bias = jnp.where(hit, 0.0, sel_bias)
        work = jnp.where(hit, -jnp.inf, work)
    sel_bias = sel_bias.astype(BF16)

    def selected(ne):
        expand = jnp.where((iota((NB, ne), 1) >> SLC_SHIFT) == iota((NB, ne), 0), 1.0, 0.0).astype(BF16)
        bias = _mm_tn(sel_bias, expand)
        bias = jnp.where(iota((TQ, ne), 1) <= t0 + iota((TQ, ne), 0), bias, NEG_INF)
        ks = ks_s[0:ne, :]
        vs = vs_s[0:ne, :]
        for r in range(R):
            s = _mm_nt(q4b[r * TQ:(r + 1) * TQ], ks) + bias
            e = jnp.exp(s - jnp.max(s, axis=-1, keepdims=True))
            den = jnp.sum(e, axis=-1, keepdims=True)
            oslc_s[r * TQ:(r + 1) * TQ, :] = jnp.dot(e.astype(BF16), vs, preferred_element_type=F32) / den

    for vi in range(S // NSA_KEY_STEP):
        pl.when((t0 >= vi * NSA_KEY_STEP) & (t0 < (vi + 1) * NSA_KEY_STEP))(
            functools.partial(selected, (vi + 1) * NSA_KEY_STEP))

    span = WIN_LEN + TQ
    start = pl.multiple_of(t0, TQ)
    kw = kw_s[pl.ds(start, span), :]
    vw = vw_s[pl.ds(start, span), :]
    rel = iota((TQ, span), 1) - WIN_LEN - iota((TQ, span), 0)
    ok_w = (rel <= 0) & (rel > -WIN_LEN) & (t0 - WIN_LEN + iota((TQ, span), 1) >= 0)
    bias_w = jnp.where(ok_w, 0.0, NEG_INF)

    sg = _sigmoid(sm_ref[...])
    sl = iota((TQ, DH), 1)
    for r in range(R):
        rows = slice(r * TQ, (r + 1) * TQ)
        s = _mm_nt(q4b[rows], kw) + bias_w
        e = jnp.exp(s - jnp.max(s, axis=-1, keepdims=True))
        den = jnp.sum(e, axis=-1, keepdims=True)
        o_win = jnp.dot(e.astype(BF16), vw, preferred_element_type=F32) / den
        base = SM_NGATE + (g * R + r) * 3
        gate = [jnp.sum(jnp.where(sl == base + i, sg, 0.0), axis=-1, keepdims=True) for i in range(3)]
        o_ref[:, r * DH:(r + 1) * DH] = (gate[0] * o_cmp[rows] + gate[1] * oslc_s[rows, :]
                                         + gate[2] * o_win).astype(o_ref.dtype)


def _nsa(u3, kc16, vc16, pe_k, pe_v, wk1, wk2, wv1, wv2):
    B, S, _ = u3.shape
    G, R, TQ = NSA_GROUPS, NSA_REP, NSA_TQ
    NC = S // CMP_STRIDE

    def kv(i):
        return pl.BlockSpec((None, S, DH), lambda b, g, t: (b, 0, OFF_NKV // DH + i * G + g))

    def full(shape):
        return pl.BlockSpec(shape, lambda b, g, t: (0,) * len(shape))

    c16 = pl.BlockSpec((None, None, NC, CMP_STRIDE * DH), lambda b, g, t: (b, g, 0, 0))
    return pl.pallas_call(
        _nsa_kernel,
        grid=(B, G, S // TQ),
        in_specs=[
            pl.BlockSpec((None, TQ, R * DH), lambda b, g, t: (b, t, OFF_NQ // (R * DH) + g)),
            c16, c16, kv(2), kv(3), kv(4), kv(5),
            pl.BlockSpec((None, TQ, DH), lambda b, g, t: (b, t, OFF_SM // DH)),
            full((2, CMP_STRIDE * DH)), full((2, CMP_STRIDE * DH)),
            full((CMP_LEN * DH, DH)), full((DH, DH)), full((CMP_LEN * DH, DH)), full((DH, DH)),
        ],
        out_specs=pl.BlockSpec((None, TQ, R * DH), lambda b, g, t: (b, t, g)),
        out_shape=jax.ShapeDtypeStruct((B, S, HEADS * DH), BF16),
        scratch_shapes=[
            pltpu.VMEM((NC, DH), F32), pltpu.VMEM((NC, DH), F32),
            pltpu.VMEM((S, DH), BF16), pltpu.VMEM((S, DH), BF16),
            pltpu.VMEM((S + WIN_LEN, DH), BF16), pltpu.VMEM((S + WIN_LEN, DH), BF16),
            pltpu.VMEM((R * TQ, DH), F32),
        ],
        compiler_params=_cparams(("parallel", "parallel", "arbitrary")),
        name="nsa",
    )(u3, kc16, vc16, u3, u3, u3, u3, u3, pe_k, pe_v, wk1, wk2, wv1, wv2)


def _merge_kernel(x_ref, ya_ref, yb_ref, oc_ref, hg_ref, hn_ref, ga_ref, gb_ref, gc_ref,
                  pa_ref, pb_ref, pc_ref, wo_ref, o_ref, yc_s):
    @pl.when(pl.program_id(1) == 0)
    def _():
        yc_s[...] = (_rms(oc_ref[...], hn_ref[...]) * _silu(hg_ref[...])).astype(BF16)
        o_ref[...] = x_ref[...]

    m = (_sigmoid(ga_ref[...]) * jnp.dot(ya_ref[...], pa_ref[...], preferred_element_type=F32)
         + _sigmoid(gb_ref[...]) * jnp.dot(yb_ref[...], pb_ref[...], preferred_element_type=F32)
         + _sigmoid(gc_ref[...]) * jnp.dot(yc_s[...], pc_ref[...], preferred_element_type=F32))
    o_ref[...] += jnp.dot(m.astype(BF16), wo_ref[...], preferred_element_type=F32)


def _merge(x, u, ya, yb, oc, hnorm, pa, pb, pc, wo, tm=512, tk=512):
    T, D = x.shape
    V = ya.shape[1]

    def gate(i):
        return pl.BlockSpec((tm, tk), lambda m, c: (m, (OFF_MG + i * D) // tk + c))

    rows = lambda w: pl.BlockSpec((tm, w), lambda m, c: (m, 0))
    proj = pl.BlockSpec((V, tk), lambda m, c: (0, c))
    return pl.pallas_call(
        _merge_kernel,
        grid=(T // tm, D // tk),
        in_specs=[
            rows(D), rows(V), rows(V), rows(V),
            pl.BlockSpec((tm, V), lambda m, c: (m, OFF_HG // V)),
            pl.BlockSpec((1, V), lambda m, c: (0, 0)),
            gate(0), gate(1), gate(2),
            proj, proj, proj,
            pl.BlockSpec((tk, D), lambda m, c: (c, 0)),
        ],
        out_specs=rows(D),
        out_shape=jax.ShapeDtypeStruct((T, D), F32),
        scratch_shapes=[pltpu.VMEM((tm, V), BF16)],
        compiler_params=_cparams(("parallel", "arbitrary")),
        name="merge",
    )(x, ya, yb, oc, u, hnorm, u, u, u, pa, pb, pc, wo)


def _relayout_w_in(w):
    D = w.shape[0]
    o = 0
    qkv = w[:, o:o + 3072]; o += 3072
    beta = w[:, o:o + 8]; o += 8
    a = w[:, o:o + 8]; o += 8
    ggate = w[:, o:o + 1024]; o += 1024
    nq = w[:, o:o + 1024]; o += 1024
    nkv = w[:, o:o + 1536]; o += 1536
    ngate = w[:, o:o + 24]; o += 24
    hg = w[:, o:o + 4096]; o += 4096
    mg = w[:, o:o + 6144]; o += 6144
    small = jnp.concatenate([beta, a, ngate, jnp.zeros((D, OFF_HQ - OFF_SM - 40), w.dtype)], axis=1)
    return jnp.concatenate([qkv, ggate, nq, nkv, small, hg, mg], axis=1)


def kernel(x, ffn1_norm, ffn1_w_gate, ffn1_w_up, ffn1_w_down, mix_norm, w_in, gdn_conv, gdn_a_log, gdn_dt_bias, gdn_out_norm, nsa_cmp_pe_k, nsa_cmp_pe_v, nsa_cmp_k_w1, nsa_cmp_k_w2, nsa_cmp_v_w1, nsa_cmp_v_w2, hgrn_lb_logits, hgrn_out_norm, w_proj_a, w_proj_b, w_proj_c, w_out, ffn2_norm, ffn2_w_gate, ffn2_w_up, ffn2_w_down, final_norm):
    B, S, D = x.shape
    L = w_in.shape[0]
    T = B * S
    G = NSA_GROUPS
    row = lambda v: v.reshape(1, -1).astype(F32)
    bf = lambda w: w.astype(BF16)
    rep = lambda v: jnp.broadcast_to(v.astype(F32).reshape(HEADS, 1, -1), (HEADS, 1, DH))

    lb_p = jax.nn.softmax(hgrn_lb_logits.astype(F32), axis=0)
    lower_bounds = jnp.cumsum(lb_p, axis=0) - lb_p[0]
    fin = row(final_norm)

    xf = x.reshape(T, D)
    for l in range(L):
        xf = _ffn(xf, row(ffn1_norm[l]), bf(ffn1_w_gate[l]), bf(ffn1_w_up[l]), bf(ffn1_w_down[l]), fin, False)

        u = _inproj(xf, row(mix_norm[l]), _relayout_w_in(bf(w_in[l])))
        u3 = u.reshape(B, S, N_U)
        ya = _gdn(u3, gdn_conv[l].astype(F32), rep(gdn_a_log[l]), rep(gdn_dt_bias[l]), row(gdn_out_norm[l]))
        kv = u3[:, :, OFF_NKV:OFF_NKV + 2 * G * DH].reshape(B, S // CMP_STRIDE, CMP_STRIDE, 2, G, DH)
        kv16 = kv.transpose(3, 0, 4, 1, 2, 5).reshape(2, B, G, S // CMP_STRIDE, CMP_STRIDE * DH)
        pe2 = lambda pe: pe.astype(F32).reshape(2, CMP_STRIDE * DH)
        yb = _nsa(u3, kv16[0], kv16[1], pe2(nsa_cmp_pe_k[l]), pe2(nsa_cmp_pe_v[l]),
                  nsa_cmp_k_w1[l], nsa_cmp_k_w2[l], nsa_cmp_v_w1[l], nsa_cmp_v_w2[l])
        oc = _hgrn(u3, lower_bounds[l].reshape(HEADS, 1, DH))
        xf = _merge(xf, u, ya.reshape(T, -1), yb.reshape(T, -1), oc.reshape(T, -1), row(hgrn_out_norm[l]),
                    bf(w_proj_a[l]), bf(w_proj_b[l]), bf(w_proj_c[l]), bf(w_out[l]))

        xf = _ffn(xf, row(ffn2_norm[l]), bf(ffn2_w_gate[l]), bf(ffn2_w_up[l]), bf(ffn2_w_down[l]), fin,
                  l == L - 1)
    return xf.reshape(B, S, D)
```

```python
import functools

import jax
import jax.numpy as jnp
from jax import lax
from jax.experimental import pallas as pl
from jax.experimental.pallas import tpu as pltpu

F32 = jnp.float32
BF16 = jnp.bfloat16
HI = lax.Precision.HIGHEST

D_MODEL = 2048
D_FF = 5632
NORM_EPS = 1e-6
NEG_INF = -1e30
FORCE = 1e6
LOG_FLOOR = 1e-30

HEADS = 8
DH = 128
CHUNK = 64
GDN_CONV = 4
NSA_GROUPS = 2
NSA_REP = HEADS // NSA_GROUPS
CMP_LEN = 32
CMP_STRIDE = 16
SLC_LEN = 64
SLC_SHIFT = 6
SLC_TOPN = 8
WIN_LEN = 512
NSA_TQ = 128
NSA_KEY_STEP = 512

OFF_GQ, OFF_GK, OFF_GV, OFF_GG = 0, 1024, 2048, 3072
OFF_NQ, OFF_NKV, OFF_SM = 4096, 5120, 6656
OFF_HQ, OFF_HF, OFF_HI, OFF_HG = 7168, 8192, 9216, 10240
OFF_MG = 11264
N_U = 17408
SM_BETA, SM_A, SM_NGATE = 0, 8, 16

VMEM_LIMIT = 56 * 1024 * 1024


def _cparams(sem):
    return pltpu.CompilerParams(dimension_semantics=sem, vmem_limit_bytes=VMEM_LIMIT)


def _sigmoid(x):
    return 1.0 / (1.0 + jnp.exp(-x))


def _silu(x):
    return x * _sigmoid(x)


def _rms(x, g):
    return x * lax.rsqrt(jnp.mean(x * x, axis=-1, keepdims=True) + NORM_EPS) * g


def _mm_nt(a, b, prec=None):
    if prec is None:
        a, b = a.astype(BF16), b.astype(BF16)
    return lax.dot_general(a, b, (((1,), (1,)), ((), ())), precision=prec, preferred_element_type=F32)


def _mm_tn(a, b, prec=None):
    if prec is None:
        a, b = a.astype(BF16), b.astype(BF16)
    return lax.dot_general(a, b, (((0,), (0,)), ((), ())), precision=prec, preferred_element_type=F32)


def _bmm(a, b):
    return jnp.einsum('cmk,ckn->cmn', a.astype(BF16), b.astype(BF16), preferred_element_type=F32)


def _bmm_nt(a, b):
    return jnp.einsum('cmk,cnk->cmn', a.astype(BF16), b.astype(BF16), preferred_element_type=F32)


def _bmm_tn(a, b):
    return jnp.einsum('ckm,ckn->cmn', a.astype(BF16), b.astype(BF16), preferred_element_type=F32)


def _split3(x):
    hi = x.astype(BF16)
    r = x - hi.astype(F32)
    mid = r.astype(BF16)
    lo = (r - mid.astype(F32)).astype(BF16)
    return hi, mid, lo


def _ffn_kernel(x_ref, g_ref, wg_ref, wu_ref, wd_ref, fg_ref, o_ref, h_ref, *, final_norm):
    f = pl.program_id(1)

    @pl.when(f == 0)
    def _():
        x = x_ref[...]
        h_ref[...] = _rms(x, g_ref[...]).astype(BF16)
        o_ref[...] = x

    h = h_ref[...]
    a = jnp.dot(h, wg_ref[...], preferred_element_type=F32)
    b = jnp.dot(h, wu_ref[...], preferred_element_type=F32)
    act = (_silu(a) * b * 0.5).astype(BF16)
    o_ref[...] += jnp.dot(act, wd_ref[...], preferred_element_type=F32)

    if final_norm:
        @pl.when(f == pl.num_programs(1) - 1)
        def _():
            o_ref[...] = _rms(o_ref[...], fg_ref[...])


def _ffn(x, g, wg, wu, wd, fg, l, final_norm, tm=1024, tf=256):
    T, D = x.shape
    F = wg.shape[2]
    return pl.pallas_call(
        functools.partial(_ffn_kernel, final_norm=final_norm),
        grid=(T // tm, F // tf),
        in_specs=[
            pl.BlockSpec((tm, D), lambda i, f: (i, 0)),
            pl.BlockSpec((1, D), lambda i, f: (0, 0)),
            pl.BlockSpec((None, D, tf), lambda i, f: (l, 0, f)),
            pl.BlockSpec((None, D, tf), lambda i, f: (l, 0, f)),
            pl.BlockSpec((None, tf, D), lambda i, f: (l, f, 0)),
            pl.BlockSpec((1, D), lambda i, f: (0, 0)),
        ],
        out_specs=pl.BlockSpec((tm, D), lambda i, f: (i, 0)),
        out_shape=jax.ShapeDtypeStruct((T, D), F32),
        scratch_shapes=[pltpu.VMEM((tm, D), BF16)],
        compiler_params=_cparams(("parallel", "arbitrary")),
        name="ffn",
    )(x, g, wg, wu, wd, fg)


def _inproj_kernel(x_ref, g_ref, w_ref, o_ref, sm_ref, h_ref, *, tn):
    n = pl.program_id(1)

    @pl.when(n == 0)
    def _():
        h_ref[...] = _rms(x_ref[...], g_ref[...]).astype(BF16)

    acc = jnp.dot(h_ref[...], w_ref[...], preferred_element_type=F32)
    o_ref[...] = acc.astype(o_ref.dtype)

    @pl.when(n == OFF_SM // tn)
    def _():
        sm_ref[...] = acc[:, OFF_SM % tn:OFF_SM % tn + DH]


def _inproj(x, g, w, l, tm=1024, tn=1024):
    T, D = x.shape
    N = w.shape[2]
    return pl.pallas_call(
        functools.partial(_inproj_kernel, tn=tn),
        grid=(T // tm, N // tn),
        in_specs=[
            pl.BlockSpec((tm, D), lambda i, n: (i, 0)),
            pl.BlockSpec((1, D), lambda i, n: (0, 0)),
            pl.BlockSpec((None, D, tn), lambda i, n: (l, 0, n)),
        ],
        out_specs=[pl.BlockSpec((tm, tn), lambda i, n: (i, n)),
                   pl.BlockSpec((tm, DH), lambda i, n: (i, 0))],
        out_shape=[jax.ShapeDtypeStruct((T, N), BF16), jax.ShapeDtypeStruct((T, DH), F32)],
        scratch_shapes=[pltpu.VMEM((tm, D), BF16)],
        compiler_params=_cparams(("parallel", "arbitrary")),
        name="in_proj",
    )(x, g, w)


def _gdn_kernel(q_ref, k_ref, v_ref, gate_ref, sm_ref, cq_ref, ck_ref, cv_ref, alog_ref, dtb_ref, ng_ref,
                o_ref, kwt_s, bt_s, sp_s, eg_s):
    h = pl.program_id(1)
    S = q_ref.shape[0]
    n = S // CHUNK
    C = CHUNK
    row = lax.broadcasted_iota(jnp.int32, (S, DH), 0)

    def conv_silu(x, w):
        y = x * w[GDN_CONV - 1:GDN_CONV]
        for j in range(1, GDN_CONV):
            xs = jnp.where(row >= j, pltpu.roll(x, j, axis=0), 0.0)
            y = y + xs * w[GDN_CONV - 1 - j:GDN_CONV - j]
        return _silu(y)

    def l2n(x):
        return x * lax.rsqrt(jnp.sum(x * x, axis=-1, keepdims=True) + 1e-6)

    q = l2n(conv_silu(q_ref[...].astype(F32), cq_ref[...])) * (DH ** -0.5)
    k = l2n(conv_silu(k_ref[...].astype(F32), ck_ref[...]))
    v = conv_silu(v_ref[...].astype(F32), cv_ref[...])

    r2 = lax.broadcasted_iota(jnp.int32, (DH, 2 * DH), 0)
    c2 = lax.broadcasted_iota(jnp.int32, (DH, 2 * DH), 1)
    onehot = jnp.where(r2 == jnp.where(c2 < DH, SM_BETA + h, SM_A + h), 1.0, 0.0).astype(BF16)
    ba = sum(jnp.dot(p, onehot, preferred_element_type=F32) for p in _split3(sm_ref[...]))
    beta = _sigmoid(ba[:, :DH])
    apre = ba[:, DH:] + dtb_ref[...]
    softplus = jnp.maximum(apre, 0.0) + jnp.log1p(jnp.exp(-jnp.abs(apre)))
    g = -jnp.exp(alog_ref[...]) * softplus

    q3, k3, v3 = (t.reshape(n, C, DH) for t in (q, k, v))
    beta3 = beta.reshape(n, C, DH)
    g3 = g.reshape(n, C, DH)

    ti = lax.broadcasted_iota(jnp.int32, (n, C, C), 1)
    si = lax.broadcasted_iota(jnp.int32, (n, C, C), 2)
    causal = ti >= si
    strict = ti > si
    tril = jnp.where(causal, 1.0, 0.0).astype(BF16)
    eye = jnp.where(ti == si, 1.0, 0.0)

    gc3 = sum(_bmm(tril, p) for p in _split3(g3))
    lane3 = lax.broadcasted_iota(jnp.int32, (n, C, DH), 2)
    ones = jnp.ones((n, C, DH), BF16)
    gc_row = sum(_bmm_nt(ones, p) for p in _split3(jnp.where(lane3 == 0, gc3, 0.0)))
    decay = jnp.where(causal, jnp.exp(jnp.where(causal, gc3[:, :, :C] - gc_row, 0.0)), 0.0)

    kb3 = k3 * beta3
    m = jnp.where(strict, _bmm_nt(kb3, k3) * decay, 0.0)
    def lower_left(blk):
        half = blk // 2
        pair = ((ti & (blk - 1)) >= half) & ((si & (blk - 1)) < half)
        if blk < C:
            pair = pair & ((ti & -blk) == (si & -blk))
        return jnp.where(pair, m, 0.0)

    x = eye - lower_left(2)
    for blk in (4, 8, 16, 32, 64):
        x = x - _bmm(_bmm(x, lower_left(blk)), x)
    egc = jnp.exp(gc3)
    uw = _bmm(x, jnp.concatenate([v3 * beta3, kb3 * egc], axis=-1))
    attn = jnp.where(causal, _bmm_nt(q3, k3) * decay, 0.0)
    g_last = gc3[:, C - 1:C, :]
    kg = k3 * jnp.exp(g_last - gc3)
    au = _bmm(attn, uw)
    o0 = au[:, :, :DH]
    qeff = q3 * egc - au[:, :, DH:]
    kwb = _bmm_tn(uw, kg)
    bt_s[...] = kwb[:, :DH, :]
    kwt_s[...] = kwb[:, DH:, :].astype(BF16)
    eg_s[...] = jnp.exp(g_last)

    def body(c, st):
        stb = st.astype(BF16)
        sp_s[c] = stb
        return st * eg_s[c] - jnp.dot(stb, kwt_s[c], preferred_element_type=F32) + bt_s[c]

    lax.fori_loop(0, n, body, jnp.zeros((DH, DH), F32))

    o = (o0 + _bmm_nt(qeff, sp_s[...])).reshape(S, DH)
    o_ref[...] = (_rms(o, ng_ref[...]) * _silu(gate_ref[...].astype(F32))).astype(o_ref.dtype)


def _gdn(u3, sm3, conv_w, a_log_b, dt_bias_b, norm_g):
    B, S, _ = u3.shape
    n = S // CHUNK

    def col(off):
        return pl.BlockSpec((None, S, DH), lambda b, h: (b, 0, off // DH + h))

    def cw(off):
        return pl.BlockSpec((GDN_CONV, DH), lambda b, h: (0, off // DH + h))

    return pl.pallas_call(
        _gdn_kernel,
        grid=(B, HEADS),
        in_specs=[
            col(OFF_GQ), col(OFF_GK), col(OFF_GV), col(OFF_GG),
            pl.BlockSpec((None, S, DH), lambda b, h: (b, 0, 0)),
            cw(0), cw(1024), cw(2048),
            pl.BlockSpec((None, 1, DH), lambda b, h: (h, 0, 0)),
            pl.BlockSpec((None, 1, DH), lambda b, h: (h, 0, 0)),
            pl.BlockSpec((1, DH), lambda b, h: (0, 0)),
        ],
        out_specs=pl.BlockSpec((None, S, DH), lambda b, h: (b, 0, h)),
        out_shape=jax.ShapeDtypeStruct((B, S, HEADS * DH), BF16),
        scratch_shapes=[
            pltpu.VMEM((n, DH, DH), BF16), pltpu.VMEM((n, DH, DH), F32),
            pltpu.VMEM((n, DH, DH), BF16), pltpu.VMEM((n, 1, DH), F32),
        ],
        compiler_params=_cparams(("parallel", "arbitrary")),
        name="gdn",
    )(u3, u3, u3, u3, sm3, conv_w, conv_w, conv_w, a_log_b, dt_bias_b, norm_g)


def _hgrn_kernel(q_ref, f_ref, i_ref, lb_ref, o_ref, kv_s, sp_s, eb_s):
    S = q_ref.shape[0]
    C = CHUNK
    n = S // C
    lb = lb_ref[...]
    q = _silu(q_ref[...].astype(F32))
    f = lb + (1.0 - lb) * _sigmoid(f_ref[...].astype(F32))
    logf = jnp.log(jnp.maximum(f, LOG_FLOOR))
    k = 1.0 - f
    v3 = i_ref[...].reshape(n, C, DH)

    ti = lax.broadcasted_iota(jnp.int32, (n, C, C), 1)
    si = lax.broadcasted_iota(jnp.int32, (n, C, C), 2)
    tril = jnp.where(ti >= si, 1.0, 0.0).astype(BF16)
    b3 = sum(_bmm(tril, p) for p in _split3(logf.reshape(n, C, DH)))
    b = b3.reshape(S, DH)
    row = lax.broadcasted_iota(jnp.int32, (S, DH), 0)

    def mid_rows(blk):
        half = blk // 2
        if blk >= 8:
            t4 = b.reshape(S // blk, blk, DH)
            return jnp.broadcast_to(t4[:, half:half + 1, :], t4.shape).reshape(S, DH)
        if blk == 4:
            d = jnp.where((row & 1) == 1, pltpu.roll(b, 1, axis=0), b)
            return jnp.where((row & 3) < 2, pltpu.roll(d, S - 2, axis=0), d)
        return jnp.where((row & 1) == 0, pltpu.roll(b, S - 1, axis=0), b)

    a = jnp.where(ti == si, jnp.sum((q * k).reshape(n, C, DH), axis=-1, keepdims=True), 0.0)
    for blk in (64, 32, 16, 8, 4, 2):
        half = blk // 2
        upper = (row & (blk - 1)) >= half
        x = (jnp.where(upper, q, k) * jnp.exp(-jnp.abs(b - mid_rows(blk)))).reshape(n, C, DH)
        pair = ((ti & (blk - 1)) >= half) & ((si & (blk - 1)) < half)
        if blk < C:
            pair = pair & ((ti & -blk) == (si & -blk))
        a = jnp.where(pair, _bmm_nt(x, x), a)

    o_intra = _bmm(a, v3)
    b_last = b3[:, C - 1:C, :]
    q3 = q.reshape(n, C, DH)
    kv_s[...] = _bmm_tn(v3, k.reshape(n, C, DH) * jnp.exp(b_last - b3))
    eb_s[...] = jnp.exp(b_last)

    def body(c, st):
        sp_s[c] = st.astype(BF16)
        return st * eb_s[c] + kv_s[c]

    lax.fori_loop(0, n, body, jnp.zeros((DH, DH), F32))
    o_ref[...] = (o_intra + _bmm_nt(q3 * jnp.exp(b3), sp_s[...])).reshape(S, DH).astype(o_ref.dtype)


def _hgrn(u3, lb_b):
    B, S, _ = u3.shape
    n = S // CHUNK

    def col(off):
        return pl.BlockSpec((None, S, DH), lambda b, h: (b, 0, off // DH + h))

    return pl.pallas_call(
        _hgrn_kernel,
        grid=(B, HEADS),
        in_specs=[col(OFF_HQ), col(OFF_HF), col(OFF_HI),
                  pl.BlockSpec((None, 1, DH), lambda b, h: (h, 0, 0))],
        out_specs=pl.BlockSpec((None, S, DH), lambda b, h: (b, 0, h)),
        out_shape=jax.ShapeDtypeStruct((B, S, HEADS * DH), BF16),
        scratch_shapes=[
            pltpu.VMEM((n, DH, DH), F32), pltpu.VMEM((n, DH, DH), BF16), pltpu.VMEM((n, 1, DH), F32),
        ],
        compiler_params=_cparams(("parallel", "arbitrary")),
        name="hgrn2",
    )(u3, u3, u3, lb_b)


def _nsa_kernel(q_ref, kc16_ref, vc16_ref, ks_ref, vs_ref, kw_ref, vw_ref, sm_ref,
                pek_ref, pev_ref, wk1_ref, wk2_ref, wv1_ref, wv2_ref,
                o_ref, kc_s, vc_s, kw_s, vw_s, oslc_s):
    g = pl.program_id(1)
    qi = pl.program_id(2)
    S = ks_ref.shape[0]
    TQ = NSA_TQ
    R = NSA_REP
    NC = S // CMP_STRIDE
    NB = S // SLC_LEN
    half = CMP_STRIDE * DH

    @pl.when(qi == 0)
    def _():
        def compress(x16_ref, pe_ref, w1_ref, w2_ref):
            x16 = x16_ref[...].astype(F32)
            pe = pe_ref[...]
            a = jnp.dot(x16 + pe[0:1], w1_ref[0:half, :], precision=HI, preferred_element_type=F32)
            b = jnp.dot(x16 + pe[1:2], w1_ref[half:, :], precision=HI, preferred_element_type=F32)
            pre = a + pltpu.roll(b, NC - 1, axis=0)
            return jnp.dot(_silu(pre), w2_ref[...], precision=HI, preferred_element_type=F32)

        kc_s[...] = compress(kc16_ref, pek_ref, wk1_ref, wk2_ref)
        vc_s[...] = compress(vc16_ref, pev_ref, wv1_ref, wv2_ref)
        kw_s[0:WIN_LEN, :] = jnp.zeros((WIN_LEN, DH), BF16)
        vw_s[0:WIN_LEN, :] = jnp.zeros((WIN_LEN, DH), BF16)
        kw_s[WIN_LEN:, :] = kw_ref[...]
        vw_s[WIN_LEN:, :] = vw_ref[...]

    t0 = qi * TQ
    qb = q_ref[...].astype(F32) * (DH ** -0.5)
    q4 = jnp.concatenate([qb[:, r * DH:(r + 1) * DH] for r in range(R)], axis=0)
    q4b = q4.astype(BF16)

    def iota(shape, dim):
        return lax.broadcasted_iota(jnp.int32, shape, dim)

    sc = _mm_nt(q4, kc_s[...], HI)
    n_id = iota((R * TQ, NC), 1)
    t_c = t0 + (iota((R * TQ, NC), 0) & (TQ - 1))
    ok_c = (n_id * CMP_STRIDE + (CMP_LEN - 1) <= t_c) & (n_id < NC - 1)
    m_c = jnp.max(jnp.where(ok_c, sc, NEG_INF), axis=-1, keepdims=True)
    e_c = jnp.where(ok_c, jnp.exp(sc - m_c), 0.0)
    den_c = jnp.sum(e_c, axis=-1, keepdims=True)
    p_c = e_c / jnp.where(den_c > 0.0, den_c, 1.0)
    o_cmp = jnp.dot(p_c, vc_s[...], precision=HI, preferred_element_type=F32)

    p_sum = p_c[0:TQ]
    for r in range(1, R):
        p_sum = p_sum + p_c[r * TQ:(r + 1) * TQ]
    js = iota((NB, NC), 0) * SLC_LEN
    cn = iota((NB, NC), 1) * CMP_STRIDE
    overlap_t = jnp.where((cn < js + SLC_LEN) & (cn + CMP_LEN > js), 1.0, 0.0)
    imp = _mm_nt(overlap_t, p_sum, HI)
    blk = iota((NB, TQ), 0)
    cur = (t0 + iota((NB, TQ), 1)) >> SLC_SHIFT
    forced = (blk == 0) | (blk == cur) | (blk == cur - 1)
    work = jnp.where(blk > cur, -FORCE, jnp.where(forced, FORCE, imp))
    blkf = blk.astype(F32)
    sel_bias = jnp.full((NB, TQ), NEG_INF, F32)
    for _ in range(SLC_TOPN):
        mx = jnp.max(work, axis=0, keepdims=True)
        idx = jnp.min(jnp.where(work == mx, blkf, float(NB)), axis=0, keepdims=True)
        hit = blkf == idx
        sel_bias = jnp.where(hit, 0.0, sel_bias)
        work = jnp.where(hit, -jnp.inf, work)
    sel_bias = sel_bias.astype(BF16)

    def selected(ne):
        expand = jnp.where((iota((NB, ne), 1) >> SLC_SHIFT) == iota((NB, ne), 0), 1.0, 0.0).astype(BF16)
        bias = _mm_tn(sel_bias, expand)
        bias = jnp.where(iota((TQ, ne), 1) <= t0 + iota((TQ, ne), 0), bias, NEG_INF)
        ks = ks_ref[0:ne, :]
        vs = vs_ref[0:ne, :]
        for r in range(R):
            s = _mm_nt(q4b[r * TQ:(r + 1) * TQ], ks) + bias
            e = jnp.exp(s - jnp.max(s, axis=-1, keepdims=True))
            den = jnp.sum(e, axis=-1, keepdims=True)
            oslc_s[r * TQ:(r + 1) * TQ, :] = jnp.dot(e.astype(BF16), vs, preferred_element_type=F32) / den

    for vi in range(S // NSA_KEY_STEP):
        pl.when((t0 >= vi * NSA_KEY_STEP) & (t0 < (vi + 1) * NSA_KEY_STEP))(
            functools.partial(selected, (vi + 1) * NSA_KEY_STEP))

    span = WIN_LEN + TQ
    start = pl.multiple_of(t0, TQ)
    kw = kw_s[pl.ds(start, span), :]
    vw = vw_s[pl.ds(start, span), :]
    rel = iota((TQ, span), 1) - WIN_LEN - iota((TQ, span), 0)
    ok_w = (rel <= 0) & (rel > -WIN_LEN) & (t0 - WIN_LEN + iota((TQ, span), 1) >= 0)
    bias_w = jnp.where(ok_w, 0.0, NEG_INF)

    sg = _sigmoid(sm_ref[...])
    sl = iota((TQ, DH), 1)
    for r in range(R):
        rows = slice(r * TQ, (r + 1) * TQ)
        s = _mm_nt(q4b[rows], kw) + bias_w
        e = jnp.exp(s - jnp.max(s, axis=-1, keepdims=True))
        den = jnp.sum(e, axis=-1, keepdims=True)
        o_win = jnp.dot(e.astype(BF16), vw, preferred_element_type=F32) / den
        base = SM_NGATE + (g * R + r) * 3
        gate = [jnp.sum(jnp.where(sl == base + i, sg, 0.0), axis=-1, keepdims=True) for i in range(3)]
        o_ref[:, r * DH:(r + 1) * DH] = (gate[0] * o_cmp[rows] + gate[1] * oslc_s[rows, :]
                                         + gate[2] * o_win).astype(o_ref.dtype)


def _nsa(u3, sm3, kc16, vc16, pe_k, pe_v, wk1, wk2, wv1, wv2):
    B, S, _ = u3.shape
    G, R, TQ = NSA_GROUPS, NSA_REP, NSA_TQ
    NC = S // CMP_STRIDE

    def kv(i):
        return pl.BlockSpec((None, S, DH), lambda b, g, t: (b, 0, OFF_NKV // DH + i * G + g))

    def full(shape):
        return pl.BlockSpec(shape, lambda b, g, t: (0,) * len(shape))

    c16 = pl.BlockSpec((None, None, NC, CMP_STRIDE * DH), lambda b, g, t: (b, g, 0, 0))
    return pl.pallas_call(
        _nsa_kernel,
        grid=(B, G, S // TQ),
        in_specs=[
            pl.BlockSpec((None, TQ, R * DH), lambda b, g, t: (b, t, OFF_NQ // (R * DH) + g)),
            c16, c16, kv(2), kv(3), kv(4), kv(5),
            pl.BlockSpec((None, TQ, DH), lambda b, g, t: (b, t, 0)),
            full((2, CMP_STRIDE * DH)), full((2, CMP_STRIDE * DH)),
            full((CMP_LEN * DH, DH)), full((DH, DH)), full((CMP_LEN * DH, DH)), full((DH, DH)),
        ],
        out_specs=pl.BlockSpec((None, TQ, R * DH), lambda b, g, t: (b, t, g)),
        out_shape=jax.ShapeDtypeStruct((B, S, HEADS * DH), BF16),
        scratch_shapes=[
            pltpu.VMEM((NC, DH), F32), pltpu.VMEM((NC, DH), F32),
            pltpu.VMEM((S + WIN_LEN, DH), BF16), pltpu.VMEM((S + WIN_LEN, DH), BF16),
            pltpu.VMEM((R * TQ, DH), F32),
        ],
        compiler_params=_cparams(("parallel", "parallel", "arbitrary")),
        name="nsa",
    )(u3, kc16, vc16, u3, u3, u3, u3, sm3, pe_k, pe_v, wk1, wk2, wv1, wv2)


def _merge_kernel(x_ref, ya_ref, yb_ref, oc_ref, hg_ref, hn_ref, ga_ref, gb_ref, gc_ref,
                  pa_ref, pb_ref, pc_ref, wo_ref, o_ref, yc_s, mg_s, *, nk):
    c = pl.program_id(1)
    tk = o_ref.shape[1]

    @pl.when(c == 0)
    def _():
        yc_s[...] = (_rms(oc_ref[...].astype(F32), hn_ref[...]) * _silu(hg_ref[...].astype(F32))).astype(BF16)

    @pl.when(c < nk)
    def _():
        m = (_sigmoid(ga_ref[...].astype(F32)) * jnp.dot(ya_ref[...], pa_ref[...], preferred_element_type=F32)
             + _sigmoid(gb_ref[...].astype(F32)) * jnp.dot(yb_ref[...], pb_ref[...], preferred_element_type=F32)
             + _sigmoid(gc_ref[...].astype(F32)) * jnp.dot(yc_s[...], pc_ref[...], preferred_element_type=F32))
        mg_s[c] = m.astype(BF16)

    @pl.when(c >= nk)
    def _():
        acc = x_ref[...]
        for j in range(nk):
            acc = acc + jnp.dot(mg_s[j], wo_ref[j * tk:(j + 1) * tk, :], preferred_element_type=F32)
        o_ref[...] = acc


def _merge(x, u, ya, yb, oc, hnorm, pa, pb, pc, wo, l, tm=1024, tk=512):
    T, D = x.shape
    V = ya.shape[1]
    nk = D // tk
    first = lambda c: jnp.minimum(c, nk - 1)
    second = lambda c: jnp.maximum(c - nk, 0)

    def gate(i):
        return pl.BlockSpec((tm, tk), lambda m, c: (m, (OFF_MG + i * D) // tk + first(c)))

    rows = lambda w: pl.BlockSpec((tm, w), lambda m, c: (m, 0))
    proj = pl.BlockSpec((None, V, tk), lambda m, c: (l, 0, first(c)))
    return pl.pallas_call(
        functools.partial(_merge_kernel, nk=nk),
        grid=(T // tm, 2 * nk),
        in_specs=[
            pl.BlockSpec((tm, tk), lambda m, c: (m, second(c))),
            rows(V), rows(V), rows(V),
            pl.BlockSpec((tm, V), lambda m, c: (m, OFF_HG // V)),
            pl.BlockSpec((1, V), lambda m, c: (0, 0)),
            gate(0), gate(1), gate(2),
            proj, proj, proj,
            pl.BlockSpec((None, D, tk), lambda m, c: (l, 0, second(c))),
        ],
        out_specs=pl.BlockSpec((tm, tk), lambda m, c: (m, second(c))),
        out_shape=jax.ShapeDtypeStruct((T, D), F32),
        scratch_shapes=[pltpu.VMEM((tm, V), BF16), pltpu.VMEM((nk, tm, tk), BF16)],
        compiler_params=_cparams(("parallel", "arbitrary")),
        name="merge",
    )(x, ya, yb, oc, u, hnorm, u, u, u, pa, pb, pc, wo)


def _relayout_w_in(w):
    o = 0
    qkv = w[..., o:o + 3072]; o += 3072
    beta = w[..., o:o + 8]; o += 8
    a = w[..., o:o + 8]; o += 8
    ggate = w[..., o:o + 1024]; o += 1024
    nq = w[..., o:o + 1024]; o += 1024
    nkv = w[..., o:o + 1536]; o += 1536
    ngate = w[..., o:o + 24]; o += 24
    hg = w[..., o:o + 4096]; o += 4096
    mg = w[..., o:o + 6144]; o += 6144
    pad = jnp.zeros(w.shape[:-1] + (OFF_HQ - OFF_SM - 40,), w.dtype)
    return jnp.concatenate([qkv, ggate, nq, nkv, beta, a, ngate, pad, hg, mg], axis=-1)


def kernel(x, ffn1_norm, ffn1_w_gate, ffn1_w_up, ffn1_w_down, mix_norm, w_in, gdn_conv, gdn_a_log, gdn_dt_bias, gdn_out_norm, nsa_cmp_pe_k, nsa_cmp_pe_v, nsa_cmp_k_w1, nsa_cmp_k_w2, nsa_cmp_v_w1, nsa_cmp_v_w2, hgrn_lb_logits, hgrn_out_norm, w_proj_a, w_proj_b, w_proj_c, w_out, ffn2_norm, ffn2_w_gate, ffn2_w_up, ffn2_w_down, final_norm):
    B, S, D = x.shape
    L = w_in.shape[0]
    T = B * S
    G = NSA_GROUPS
    row = lambda v: v.reshape(1, -1).astype(F32)
    bf = lambda w: w.astype(BF16)
    rep = lambda v: jnp.broadcast_to(v.astype(F32).reshape(HEADS, 1, -1), (HEADS, 1, DH))

    lb_p = jax.nn.softmax(hgrn_lb_logits.astype(F32), axis=0)
    lower_bounds = jnp.cumsum(lb_p, axis=0) - lb_p[0]
    fin = row(final_norm)
    f1 = (bf(ffn1_w_gate), bf(ffn1_w_up), bf(ffn1_w_down))
    f2 = (bf(ffn2_w_gate), bf(ffn2_w_up), bf(ffn2_w_down))
    w_in_r = _relayout_w_in(bf(w_in))
    projs = (bf(w_proj_a), bf(w_proj_b), bf(w_proj_c), bf(w_out))

    xf = x.reshape(T, D)
    for l in range(L):
        xf = _ffn(xf, row(ffn1_norm[l]), *f1, fin, l, False)

        u, sm = _inproj(xf, row(mix_norm[l]), w_in_r, l)
        u3 = u.reshape(B, S, N_U)
        sm3 = sm.reshape(B, S, DH)
        ya = _gdn(u3, sm3, gdn_conv[l].astype(F32), rep(gdn_a_log[l]), rep(gdn_dt_bias[l]),
                  row(gdn_out_norm[l]))
        kv = u3[:, :, OFF_NKV:OFF_NKV + 2 * G * DH].reshape(B, S // CMP_STRIDE, CMP_STRIDE, 2, G, DH)
        kv16 = kv.transpose(3, 0, 4, 1, 2, 5).reshape(2, B, G, S // CMP_STRIDE, CMP_STRIDE * DH)
        pe2 = lambda pe: pe.astype(F32).reshape(2, CMP_STRIDE * DH)
        yb = _nsa(u3, sm3, kv16[0], kv16[1], pe2(nsa_cmp_pe_k[l]), pe2(nsa_cmp_pe_v[l]),
                  nsa_cmp_k_w1[l], nsa_cmp_k_w2[l], nsa_cmp_v_w1[l], nsa_cmp_v_w2[l])
        oc = _hgrn(u3, lower_bounds[l].reshape(HEADS, 1, DH))
        xf = _merge(xf, u, ya.reshape(T, -1), yb.reshape(T, -1), oc.reshape(T, -1), row(hgrn_out_norm[l]),
                    *projs, l)

        xf = _ffn(xf, row(ffn2_norm[l]), *f2, fin, l, l == L - 1)
    return xf.reshape(B, S, D)
```

```python
import functools

import jax
import jax.numpy as jnp
from jax import lax
from jax.experimental import pallas as pl
from jax.experimental.pallas import tpu as pltpu

F32 = jnp.float32
BF16 = jnp.bfloat16

D_MODEL = 2048
D_FF = 5632
NORM_EPS = 1e-6
NEG_INF = -1e30
FORCE = 1e6
LOG_FLOOR = 1e-30

HEADS = 8
DH = 128
CHUNK = 64
GDN_CONV = 4
GDN_HPS = 2
NSA_GROUPS = 2
NSA_REP = HEADS // NSA_GROUPS
CMP_LEN = 32
CMP_STRIDE = 16
SLC_LEN = 64
SLC_SHIFT = 6
SLC_TOPN = 8
WIN_LEN = 512
NSA_TQ = 128
NSA_KEY_STEP = 512

OFF_GQ, OFF_GK, OFF_GV, OFF_GG = 0, 1024, 2048, 3072
OFF_NQ, OFF_NKV, OFF_SM = 4096, 5120, 6656
OFF_HQ, OFF_HF, OFF_HI, OFF_HG = 7168, 8192, 9216, 10240
OFF_MG = 11264
N_U = 17408
SM_BETA, SM_A, SM_NGATE = 0, 8, 16

VMEM_LIMIT = 56 * 1024 * 1024


def _cparams(sem):
    return pltpu.CompilerParams(dimension_semantics=sem, vmem_limit_bytes=VMEM_LIMIT)


def _sigmoid(x):
    return 1.0 / (1.0 + jnp.exp(-x))


def _silu(x):
    return x * _sigmoid(x)


def _rms(x, g):
    return x * lax.rsqrt(jnp.mean(x * x, axis=-1, keepdims=True) + NORM_EPS) * g


def _mm_nt(a, b):
    return lax.dot_general(a.astype(BF16), b.astype(BF16), (((1,), (1,)), ((), ())),
                           preferred_element_type=F32)


def _mm_tn(a, b):
    return lax.dot_general(a.astype(BF16), b.astype(BF16), (((0,), (0,)), ((), ())),
                           preferred_element_type=F32)


def _bmm(a, b):
    return jnp.einsum('cmk,ckn->cmn', a.astype(BF16), b.astype(BF16), preferred_element_type=F32)


def _bmm_nt(a, b):
    return jnp.einsum('cmk,cnk->cmn', a.astype(BF16), b.astype(BF16), preferred_element_type=F32)


def _bmm_tn(a, b):
    return jnp.einsum('ckm,ckn->cmn', a.astype(BF16), b.astype(BF16), preferred_element_type=F32)


def _split2(x):
    hi = x.astype(BF16)
    return hi, (x - hi.astype(F32)).astype(BF16)


def _mm_nt3(a, b):
    (ah, al), (bh, bl) = _split2(a), _split2(b)
    return _mm_nt(ah, bh) + (_mm_nt(ah, bl) + _mm_nt(al, bh))


def _mm3(a, b):
    (ah, al), (bh, bl) = _split2(a), _split2(b)
    d = lambda p, r: jnp.dot(p, r, preferred_element_type=F32)
    return d(ah, bh) + (d(ah, bl) + d(al, bh))


def _ffn_kernel(x_ref, g_ref, wg_ref, wu_ref, wd_ref, fg_ref, o_ref, h_ref, *, final_norm):
    f = pl.program_id(1)

    @pl.when(f == 0)
    def _():
        x = x_ref[...]
        h_ref[...] = _rms(x, g_ref[...]).astype(BF16)
        o_ref[...] = x

    h = h_ref[...]
    a = jnp.dot(h, wg_ref[...], preferred_element_type=F32)
    b = jnp.dot(h, wu_ref[...], preferred_element_type=F32)
    act = (_silu(a) * b * 0.5).astype(BF16)
    o_ref[...] += jnp.dot(act, wd_ref[...], preferred_element_type=F32)

    if final_norm:
        @pl.when(f == pl.num_programs(1) - 1)
        def _():
            o_ref[...] = _rms(o_ref[...], fg_ref[...])


def _ffn(x, g, wg, wu, wd, fg, l, final_norm, tm=1024, tf=256):
    T, D = x.shape
    F = wg.shape[2]
    return pl.pallas_call(
        functools.partial(_ffn_kernel, final_norm=final_norm),
        grid=(T // tm, F // tf),
        in_specs=[
            pl.BlockSpec((tm, D), lambda i, f: (i, 0)),
            pl.BlockSpec((1, D), lambda i, f: (0, 0)),
            pl.BlockSpec((None, D, tf), lambda i, f: (l, 0, f)),
            pl.BlockSpec((None, D, tf), lambda i, f: (l, 0, f)),
            pl.BlockSpec((None, tf, D), lambda i, f: (l, f, 0)),
            pl.BlockSpec((1, D), lambda i, f: (0, 0)),
        ],
        out_specs=pl.BlockSpec((tm, D), lambda i, f: (i, 0)),
        out_shape=jax.ShapeDtypeStruct((T, D), F32),
        scratch_shapes=[pltpu.VMEM((tm, D), BF16)],
        compiler_params=_cparams(("parallel", "arbitrary")),
        name="ffn",
    )(x, g, wg, wu, wd, fg)


def _inproj_kernel(x_ref, g_ref, w_ref, alog_ref, dtb_ref, o_ref, sm_ref, h_ref, *, tn):
    n = pl.program_id(1)

    @pl.when(n == 0)
    def _():
        h_ref[...] = _rms(x_ref[...], g_ref[...]).astype(BF16)

    acc = jnp.dot(h_ref[...], w_ref[...], preferred_element_type=F32)
    o_ref[...] = acc.astype(o_ref.dtype)

    @pl.when(n == OFF_SM // tn)
    def _():
        sm = acc[:, OFF_SM % tn:OFF_SM % tn + DH]
        lane = lax.broadcasted_iota(jnp.int32, (1, DH), 1)
        z = sm + dtb_ref[...]
        softplus = jnp.maximum(z, 0.0) + jnp.log1p(jnp.exp(-jnp.abs(z)))
        sm_ref[...] = jnp.where((lane >= SM_A) & (lane < SM_A + HEADS),
                                -jnp.exp(alog_ref[...]) * softplus, _sigmoid(sm))


def _inproj(x, g, w, a_log_row, dt_bias_row, l, tm=1024, tn=1024):
    T, D = x.shape
    N = w.shape[2]
    return pl.pallas_call(
        functools.partial(_inproj_kernel, tn=tn),
        grid=(T // tm, N // tn),
        in_specs=[
            pl.BlockSpec((tm, D), lambda i, n: (i, 0)),
            pl.BlockSpec((1, D), lambda i, n: (0, 0)),
            pl.BlockSpec((None, D, tn), lambda i, n: (l, 0, n)),
            pl.BlockSpec((1, DH), lambda i, n: (0, 0)),
            pl.BlockSpec((1, DH), lambda i, n: (0, 0)),
        ],
        out_specs=[pl.BlockSpec((tm, tn), lambda i, n: (i, n)),
                   pl.BlockSpec((tm, DH), lambda i, n: (i, 0))],
        out_shape=[jax.ShapeDtypeStruct((T, N), BF16), jax.ShapeDtypeStruct((T, DH), F32)],
        scratch_shapes=[pltpu.VMEM((tm, D), BF16)],
        compiler_params=_cparams(("parallel", "arbitrary")),
        name="in_proj",
    )(x, g, w, a_log_row, dt_bias_row)


def _gdn_kernel(q_ref, k_ref, v_ref, gate_ref, sm_ref, cq_ref, ck_ref, cv_ref, ng_ref,
                o_ref, kwt_s, bt_s, sp_s, eg_s, o0_s, qe_s):
    hp = pl.program_id(1)
    S = q_ref.shape[0]
    n = S // CHUNK
    C = CHUNK
    row8 = lax.broadcasted_iota(jnp.int32, (8, DH), 0)
    ti = lax.broadcasted_iota(jnp.int32, (1, C, C), 1)
    si = lax.broadcasted_iota(jnp.int32, (1, C, C), 2)
    lane = lax.broadcasted_iota(jnp.int32, (1, 1, DH), 2)
    causal = ti >= si
    strict = ti > si
    tril = jnp.broadcast_to(jnp.where(causal, 1.0, 0.0).astype(BF16), (n, C, C))
    eye = jnp.where(ti == si, 1.0, 0.0)

    def conv_silu(x, w):
        taps = [w[GDN_CONV - 1 - j:GDN_CONV - j] for j in range(GDN_CONV)]
        y = x * taps[0]
        for j in range(1, GDN_CONV):
            y = y + pltpu.roll(x, j, axis=0) * taps[j]
        x8 = x[0:8]
        y8 = x8 * taps[0]
        for j in range(1, GDN_CONV):
            y8 = y8 + jnp.where(row8 >= j, pltpu.roll(x8, j, axis=0), 0.0) * taps[j]
        return _silu(jnp.concatenate([y8, y[8:]], axis=0))

    def l2n(x):
        return x * lax.rsqrt(jnp.sum(x * x, axis=-1, keepdims=True) + 1e-6)

    def prepare(i):
        cols = slice(i * DH, (i + 1) * DH)
        h = hp * GDN_HPS + i
        q = l2n(conv_silu(q_ref[:, cols].astype(F32), cq_ref[:, cols])) * (DH ** -0.5)
        k = l2n(conv_silu(k_ref[:, cols].astype(F32), ck_ref[:, cols]))
        v = conv_silu(v_ref[:, cols].astype(F32), cv_ref[:, cols])
        q3, k3, v3 = (t.reshape(n, C, DH) for t in (q, k, v))

        sm3 = sm_ref[...].reshape(n, C, DH)
        beta3 = jnp.sum(jnp.where(lane == SM_BETA + h, sm3, 0.0), axis=-1, keepdims=True)
        g3 = jnp.sum(jnp.where(lane == SM_A + h, sm3, 0.0), axis=-1, keepdims=True)

        gc3 = sum(_bmm(tril, p) for p in _split2(jnp.broadcast_to(g3, (n, C, DH))))
        gc_row = jnp.swapaxes(gc3, 1, 2)[:, :C, :]
        decay = jnp.where(causal, jnp.exp(jnp.where(causal, gc3[:, :, :C] - gc_row, 0.0)), 0.0)

        kb3 = k3 * beta3
        m = jnp.where(strict, _bmm_nt(kb3, k3) * decay, 0.0)

        def lower_left(blk):
            half = blk // 2
            pair = ((ti & (blk - 1)) >= half) & ((si & (blk - 1)) < half)
            if blk < C:
                pair = pair & ((ti & -blk) == (si & -blk))
            return jnp.where(pair, m, 0.0)

        x = eye - lower_left(2)
        for blk in (4, 8, 16, 32, 64):
            x = x - _bmm(_bmm(x, lower_left(blk)), x)
        egc = jnp.exp(gc3)
        uw = _bmm(x, jnp.concatenate([v3 * beta3, kb3 * egc], axis=-1))
        attn = jnp.where(causal, _bmm_nt(q3, k3) * decay, 0.0)
        g_last = gc3[:, C - 1:C, :]
        kg = k3 * jnp.exp(g_last - gc3)
        au = _bmm(attn, uw)
        o0_s[i] = au[:, :, :DH]
        qe_s[i] = (q3 * egc - au[:, :, DH:]).astype(BF16)
        kwb = _bmm_tn(uw, kg)
        bt_s[i] = kwb[:, :DH, :]
        kwt_s[i] = kwb[:, DH:, :].astype(BF16)
        eg_s[i] = jnp.exp(g_last)

    for i in range(GDN_HPS):
        prepare(i)

    def body(c, sts):
        nxt = []
        for i, st in enumerate(sts):
            stb = st.astype(BF16)
            sp_s[i, c] = stb
            nxt.append(st * eg_s[i, c] - jnp.dot(stb, kwt_s[i, c], preferred_element_type=F32) + bt_s[i, c])
        return tuple(nxt)

    lax.fori_loop(0, n, body, tuple(jnp.zeros((DH, DH), F32) for _ in range(GDN_HPS)))

    for i in range(GDN_HPS):
        cols = slice(i * DH, (i + 1) * DH)
        o = (o0_s[i] + _bmm_nt(qe_s[i], sp_s[i])).reshape(S, DH)
        o_ref[:, cols] = (_rms(o, ng_ref[...]) * _silu(gate_ref[:, cols].astype(F32))).astype(o_ref.dtype)


def _gdn(u3, sm3, conv_w, norm_g):
    B, S, _ = u3.shape
    n = S // CHUNK
    W = GDN_HPS * DH

    def col(off):
        return pl.BlockSpec((None, S, W), lambda b, h: (b, 0, off // W + h))

    def cw(off):
        return pl.BlockSpec((GDN_CONV, W), lambda b, h: (0, off // W + h))

    return pl.pallas_call(
        _gdn_kernel,
        grid=(B, HEADS // GDN_HPS),
        in_specs=[
            col(OFF_GQ), col(OFF_GK), col(OFF_GV), col(OFF_GG),
            pl.BlockSpec((None, S, DH), lambda b, h: (b, 0, 0)),
            cw(0), cw(1024), cw(2048),
            pl.BlockSpec((1, DH), lambda b, h: (0, 0)),
        ],
        out_specs=pl.BlockSpec((None, S, W), lambda b, h: (b, 0, h)),
        out_shape=jax.ShapeDtypeStruct((B, S, HEADS * DH), BF16),
        scratch_shapes=[
            pltpu.VMEM((GDN_HPS, n, DH, DH), BF16), pltpu.VMEM((GDN_HPS, n, DH, DH), F32),
            pltpu.VMEM((GDN_HPS, n, DH, DH), BF16), pltpu.VMEM((GDN_HPS, n, 1, DH), F32),
            pltpu.VMEM((GDN_HPS, n, CHUNK, DH), F32), pltpu.VMEM((GDN_HPS, n, CHUNK, DH), BF16),
        ],
        compiler_params=_cparams(("parallel", "arbitrary")),
        name="gdn",
    )(u3, u3, u3, u3, sm3, conv_w, conv_w, conv_w, norm_g)


def _hgrn_kernel(q_ref, f_ref, i_ref, lb_ref, o_ref, kv_s, sp_s, eb_s):
    S = q_ref.shape[0]
    C = CHUNK
    n = S // C
    lb = lb_ref[...]
    q3 = _silu(q_ref[...].astype(F32)).reshape(n, C, DH)
    f = lb + (1.0 - lb) * _sigmoid(f_ref[...].astype(F32))
    logf = jnp.log(jnp.maximum(f, LOG_FLOOR))
    k3 = (1.0 - f).reshape(n, C, DH)
    v3 = i_ref[...].reshape(n, C, DH)

    ti = lax.broadcasted_iota(jnp.int32, (1, C, C), 1)
    si = lax.broadcasted_iota(jnp.int32, (1, C, C), 2)
    row = lax.broadcasted_iota(jnp.int32, (1, C, DH), 1)
    tril = jnp.broadcast_to(jnp.where(ti >= si, 1.0, 0.0).astype(BF16), (n, C, C))
    b3 = sum(_bmm(tril, p) for p in _split2(logf.reshape(n, C, DH)))
    b = b3.reshape(S, DH)

    def shifted(t, shift):
        return pltpu.roll(t.reshape(S, DH), shift, axis=0).reshape(n, C, DH)

    def mid_rows(blk):
        half = blk // 2
        if blk >= 8:
            t4 = b.reshape(S // blk, blk, DH)
            return jnp.broadcast_to(t4[:, half:half + 1, :], t4.shape).reshape(n, C, DH)
        if blk == 4:
            d = jnp.where((row & 1) == 1, shifted(b3, 1), b3)
            return jnp.where((row & 3) < 2, shifted(d, S - 2), d)
        return jnp.where((row & 1) == 0, shifted(b3, S - 1), b3)

    a = jnp.where(ti == si, jnp.sum(q3 * k3, axis=-1, keepdims=True), 0.0)
    for blk in (64, 32, 16, 8, 4, 2):
        half = blk // 2
        upper = (row & (blk - 1)) >= half
        x = jnp.where(upper, q3, k3) * jnp.exp(-jnp.abs(b3 - mid_rows(blk)))
        pair = ((ti & (blk - 1)) >= half) & ((si & (blk - 1)) < half)
        if blk < C:
            pair = pair & ((ti & -blk) == (si & -blk))
        a = jnp.where(pair, _bmm_nt(x, x), a)

    o_intra = _bmm(a, v3)
    b_last = b3[:, C - 1:C, :]
    kv_s[...] = _bmm_tn(v3, k3 * jnp.exp(b_last - b3))
    eb_s[...] = jnp.exp(b_last)

    def body(c, st):
        sp_s[c] = st.astype(BF16)
        return st * eb_s[c] + kv_s[c]

    lax.fori_loop(0, n, body, jnp.zeros((DH, DH), F32))
    o_ref[...] = (o_intra + _bmm_nt(q3 * jnp.exp(b3), sp_s[...])).reshape(S, DH).astype(o_ref.dtype)


def _hgrn(u3, lb_b):
    B, S, _ = u3.shape
    n = S // CHUNK

    def col(off):
        return pl.BlockSpec((None, S, DH), lambda b, h: (b, 0, off // DH + h))

    return pl.pallas_call(
        _hgrn_kernel,
        grid=(B, HEADS),
        in_specs=[col(OFF_HQ), col(OFF_HF), col(OFF_HI),
                  pl.BlockSpec((None, 1, DH), lambda b, h: (h, 0, 0))],
        out_specs=pl.BlockSpec((None, S, DH), lambda b, h: (b, 0, h)),
        out_shape=jax.ShapeDtypeStruct((B, S, HEADS * DH), BF16),
        scratch_shapes=[
            pltpu.VMEM((n, DH, DH), F32), pltpu.VMEM((n, DH, DH), BF16), pltpu.VMEM((n, 1, DH), F32),
        ],
        compiler_params=_cparams(("parallel", "arbitrary")),
        name="hgrn2",
    )(u3, u3, u3, lb_b)


def _nsa_kernel(q_ref, kc_ref, vc_ref, ks_ref, vs_ref, kw_ref, vw_ref, sm_ref,
                pek_ref, pev_ref, wk1_ref, wk2_ref, wv1_ref, wv2_ref,
                o_ref, x_s, kc_s, vc_s, kw_s, vw_s, oslc_s):
    g = pl.program_id(1)
    qi = pl.program_id(2)
    S = ks_ref.shape[0]
    TQ = NSA_TQ
    R = NSA_REP
    NC = S // CMP_STRIDE
    NB = S // SLC_LEN

    @pl.when(qi == 0)
    def _():
        def compress(x_ref, pe_ref, w1_ref, w2_ref):
            x_s[...] = x_ref[...].astype(F32)
            a = jnp.zeros((NC, DH), F32)
            b = jnp.zeros((NC, DH), F32)
            for j in range(CMP_STRIDE):
                xj = x_s[pl.ds(j, NC, stride=CMP_STRIDE), :]
                jb = CMP_STRIDE + j
                a = a + _mm3(xj + pe_ref[j:j + 1, :], w1_ref[j * DH:(j + 1) * DH, :])
                b = b + _mm3(xj + pe_ref[jb:jb + 1, :], w1_ref[jb * DH:(jb + 1) * DH, :])
            pre = a + pltpu.roll(b, NC - 1, axis=0)
            return _mm3(_silu(pre), w2_ref[...])

        kc_s[...] = compress(kc_ref, pek_ref, wk1_ref, wk2_ref)
        vc_s[...] = compress(vc_ref, pev_ref, wv1_ref, wv2_ref).astype(BF16)
        kw_s[0:WIN_LEN, :] = jnp.zeros((WIN_LEN, DH), BF16)
        vw_s[0:WIN_LEN, :] = jnp.zeros((WIN_LEN, DH), BF16)
        kw_s[WIN_LEN:, :] = kw_ref[...]
        vw_s[WIN_LEN:, :] = vw_ref[...]

    t0 = qi * TQ
    qb = q_ref[...].astype(F32) * (DH ** -0.5)
    q4 = jnp.concatenate([qb[:, r * DH:(r + 1) * DH] for r in range(R)], axis=0)
    q4b = q4.astype(BF16)

    def iota(shape, dim):
        return lax.broadcasted_iota(jnp.int32, shape, dim)

    sc = _mm_nt3(q4, kc_s[...])
    n_id = iota((R * TQ, NC), 1)
    t_c = t0 + (iota((R * TQ, NC), 0) & (TQ - 1))
    ok_c = (n_id * CMP_STRIDE + (CMP_LEN - 1) <= t_c) & (n_id < NC - 1)
    m_c = jnp.max(jnp.where(ok_c, sc, NEG_INF), axis=-1, keepdims=True)
    e_c = jnp.where(ok_c, jnp.exp(sc - m_c), 0.0)
    den_c = jnp.sum(e_c, axis=-1, keepdims=True)
    p_c = e_c / jnp.where(den_c > 0.0, den_c, 1.0)
    o_cmp = jnp.dot(p_c.astype(BF16), vc_s[...], preferred_element_type=F32)

    p_sum = p_c[0:TQ]
    for r in range(1, R):
        p_sum = p_sum + p_c[r * TQ:(r + 1) * TQ]
    js = iota((NB, NC), 0) * SLC_LEN
    cn = iota((NB, NC), 1) * CMP_STRIDE
    overlap_t = jnp.where((cn < js + SLC_LEN) & (cn + CMP_LEN > js), 1.0, 0.0).astype(BF16)
    imp = sum(_mm_nt(overlap_t, p) for p in _split2(p_sum))
    blk = iota((NB, TQ), 0)
    cur = (t0 + iota((NB, TQ), 1)) >> SLC_SHIFT
    forced = (blk == 0) | (blk == cur) | (blk == cur - 1)
    work = jnp.where(blk > cur, -FORCE, jnp.where(forced, FORCE, imp))
    blkf = blk.astype(F32)
    sel_bias = jnp.full((NB, TQ), NEG_INF, F32)
    for _ in range(SLC_TOPN):
        mx = jnp.max(work, axis=0, keepdims=True)
        idx = jnp.min(jnp.where(work == mx, blkf, float(NB)), axis=0, keepdims=True)
        hit = blkf == idx
        sel_bias = jnp.where(hit, 0.0, sel_bias)
        work = jnp.where(hit, -jnp.inf, work)
    sel_bias = sel_bias.astype(BF16)

    def selected(ne):
        expand = jnp.where((iota((NB, ne), 1) >> SLC_SHIFT) == iota((NB, ne), 0), 1.0, 0.0).astype(BF16)
        bias = _mm_tn(sel_bias, expand)
        bias = jnp.where(iota((TQ, ne), 1) <= t0 + iota((TQ, ne), 0), bias, NEG_INF)
        ks = ks_ref[0:ne, :]
        vs = vs_ref[0:ne, :]
        for r in range(R):
            s = _mm_nt(q4b[r * TQ:(r + 1) * TQ], ks) + bias
            e = jnp.exp(s - jnp.max(s, axis=-1, keepdims=True))
            den = jnp.sum(e, axis=-1, keepdims=True)
            oslc_s[r * TQ:(r + 1) * TQ, :] = jnp.dot(e.astype(BF16), vs, preferred_element_type=F32) / den

    for vi in range(S // NSA_KEY_STEP):
        pl.when((t0 >= vi * NSA_KEY_STEP) & (t0 < (vi + 1) * NSA_KEY_STEP))(
            functools.partial(selected, (vi + 1) * NSA_KEY_STEP))

    span = WIN_LEN + TQ
    start = pl.multiple_of(t0, TQ)
    kw = kw_s[pl.ds(start, span), :]
    vw = vw_s[pl.ds(start, span), :]
    rel = iota((TQ, span), 1) - WIN_LEN - iota((TQ, span), 0)
    ok_w = (rel <= 0) & (rel > -WIN_LEN) & (t0 - WIN_LEN + iota((TQ, span), 1) >= 0)
    bias_w = jnp.where(ok_w, 0.0, NEG_INF)

    sg = sm_ref[...]
    sl = iota((1, DH), 1)
    for r in range(R):
        rows = slice(r * TQ, (r + 1) * TQ)
        s = _mm_nt(q4b[rows], kw) + bias_w
        e = jnp.exp(s - jnp.max(s, axis=-1, keepdims=True))
        den = jnp.sum(e, axis=-1, keepdims=True)
        o_win = jnp.dot(e.astype(BF16), vw, preferred_element_type=F32) / den
        base = SM_NGATE + (g * R + r) * 3
        gate = [jnp.sum(jnp.where(sl == base + i, sg, 0.0), axis=-1, keepdims=True) for i in range(3)]
        o_ref[:, r * DH:(r + 1) * DH] = (gate[0] * o_cmp[rows] + gate[1] * oslc_s[rows, :]
                                         + gate[2] * o_win).astype(o_ref.dtype)


def _nsa(u3, sm3, pe_k, pe_v, wk1, wk2, wv1, wv2):
    B, S, _ = u3.shape
    G, R, TQ = NSA_GROUPS, NSA_REP, NSA_TQ
    NC = S // CMP_STRIDE

    def kv(i):
        return pl.BlockSpec((None, S, DH), lambda b, g, t: (b, 0, OFF_NKV // DH + i * G + g))

    def full(shape):
        return pl.BlockSpec(shape, lambda b, g, t: (0,) * len(shape))

    return pl.pallas_call(
        _nsa_kernel,
        grid=(B, G, S // TQ),
        in_specs=[
            pl.BlockSpec((None, TQ, R * DH), lambda b, g, t: (b, t, OFF_NQ // (R * DH) + g)),
            kv(0), kv(1), kv(2), kv(3), kv(4), kv(5),
            pl.BlockSpec((None, TQ, DH), lambda b, g, t: (b, t, 0)),
            full((CMP_LEN, DH)), full((CMP_LEN, DH)),
            full((CMP_LEN * DH, DH)), full((DH, DH)), full((CMP_LEN * DH, DH)), full((DH, DH)),
        ],
        out_specs=pl.BlockSpec((None, TQ, R * DH), lambda b, g, t: (b, t, g)),
        out_shape=jax.ShapeDtypeStruct((B, S, HEADS * DH), BF16),
        scratch_shapes=[
            pltpu.VMEM((S, DH), F32),
            pltpu.VMEM((NC, DH), F32), pltpu.VMEM((NC, DH), BF16),
            pltpu.VMEM((S + WIN_LEN, DH), BF16), pltpu.VMEM((S + WIN_LEN, DH), BF16),
            pltpu.VMEM((R * TQ, DH), F32),
        ],
        compiler_params=_cparams(("parallel", "parallel", "arbitrary")),
        name="nsa",
    )(u3, u3, u3, u3, u3, u3, u3, sm3, pe_k, pe_v, wk1, wk2, wv1, wv2)


def _merge_kernel(x_ref, ya_ref, yb_ref, oc_ref, hg_ref, hn_ref, ga_ref, gb_ref, gc_ref,
                  pa_ref, pb_ref, pc_ref, wo_ref, o_ref, yc_s, mg_s, *, nk):
    c = pl.program_id(1)
    tk = o_ref.shape[1]

    @pl.when(c == 0)
    def _():
        yc_s[...] = (_rms(oc_ref[...].astype(F32), hn_ref[...]) * _silu(hg_ref[...].astype(F32))).astype(BF16)

    @pl.when(c < nk)
    def _():
        m = (_sigmoid(ga_ref[...].astype(F32)) * jnp.dot(ya_ref[...], pa_ref[...], preferred_element_type=F32)
             + _sigmoid(gb_ref[...].astype(F32)) * jnp.dot(yb_ref[...], pb_ref[...], preferred_element_type=F32)
             + _sigmoid(gc_ref[...].astype(F32)) * jnp.dot(yc_s[...], pc_ref[...], preferred_element_type=F32))
        mg_s[c] = m.astype(BF16)

    @pl.when(c >= nk)
    def _():
        acc = x_ref[...]
        for j in range(nk):
            acc = acc + jnp.dot(mg_s[j], wo_ref[j * tk:(j + 1) * tk, :], preferred_element_type=F32)
        o_ref[...] = acc


def _merge(x, u, ya, yb, oc, hnorm, pa, pb, pc, wo, l, tm=1024, tk=512):
    T, D = x.shape
    V = ya.shape[1]
    nk = D // tk
    first = lambda c: jnp.minimum(c, nk - 1)
    second = lambda c: jnp.maximum(c - nk, 0)

    def gate(i):
        return pl.BlockSpec((tm, tk), lambda m, c: (m, (OFF_MG + i * D) // tk + first(c)))

    rows = lambda w: pl.BlockSpec((tm, w), lambda m, c: (m, 0))
    proj = pl.BlockSpec((None, V, tk), lambda m, c: (l, 0, first(c)))
    return pl.pallas_call(
        functools.partial(_merge_kernel, nk=nk),
        grid=(T // tm, 2 * nk),
        in_specs=[
            pl.BlockSpec((tm, tk), lambda m, c: (m, second(c))),
            rows(V), rows(V), rows(V),
            pl.BlockSpec((tm, V), lambda m, c: (m, OFF_HG // V)),
            pl.BlockSpec((1, V), lambda m, c: (0, 0)),
            gate(0), gate(1), gate(2),
            proj, proj, proj,
            pl.BlockSpec((None, D, tk), lambda m, c: (l, 0, second(c))),
        ],
        out_specs=pl.BlockSpec((tm, tk), lambda m, c: (m, second(c))),
        out_shape=jax.ShapeDtypeStruct((T, D), F32),
        scratch_shapes=[pltpu.VMEM((tm, V), BF16), pltpu.VMEM((nk, tm, tk), BF16)],
        compiler_params=_cparams(("parallel", "arbitrary")),
        name="merge",
    )(x, ya, yb, oc, u, hnorm, u, u, u, pa, pb, pc, wo)


def _relayout_w_in(w):
    o = 0
    qkv = w[..., o:o + 3072]; o += 3072
    beta = w[..., o:o + 8]; o += 8
    a = w[..., o:o + 8]; o += 8
    ggate = w[..., o:o + 1024]; o += 1024
    nq = w[..., o:o + 1024]; o += 1024
    nkv = w[..., o:o + 1536]; o += 1536
    ngate = w[..., o:o + 24]; o += 24
    hg = w[..., o:o + 4096]; o += 4096
    mg = w[..., o:o + 6144]; o += 6144
    pad = jnp.zeros(w.shape[:-1] + (OFF_HQ - OFF_SM - 40,), w.dtype)
    return jnp.concatenate([qkv, ggate, nq, nkv, beta, a, ngate, pad, hg, mg], axis=-1)


def kernel(x, ffn1_norm, ffn1_w_gate, ffn1_w_up, ffn1_w_down, mix_norm, w_in, gdn_conv, gdn_a_log, gdn_dt_bias, gdn_out_norm, nsa_cmp_pe_k, nsa_cmp_pe_v, nsa_cmp_k_w1, nsa_cmp_k_w2, nsa_cmp_v_w1, nsa_cmp_v_w2, hgrn_lb_logits, hgrn_out_norm, w_proj_a, w_proj_b, w_proj_c, w_out, ffn2_norm, ffn2_w_gate, ffn2_w_up, ffn2_w_down, final_norm):
    B, S, D = x.shape
    L = w_in.shape[0]
    T = B * S
    row = lambda v: v.reshape(1, -1).astype(F32)
    bf = lambda w: w.astype(BF16)
    small_row = lambda v: jnp.zeros((1, DH), F32).at[0, SM_A:SM_A + HEADS].set(v.astype(F32))

    lb_p = jax.nn.softmax(hgrn_lb_logits.astype(F32), axis=0)
    lower_bounds = jnp.cumsum(lb_p, axis=0) - lb_p[0]
    fin = row(final_norm)
    f1 = (bf(ffn1_w_gate), bf(ffn1_w_up), bf(ffn1_w_down))
    f2 = (bf(ffn2_w_gate), bf(ffn2_w_up), bf(ffn2_w_down))
    w_in_r = _relayout_w_in(bf(w_in))
    projs = (bf(w_proj_a), bf(w_proj_b), bf(w_proj_c), bf(w_out))

    xf = x.reshape(T, D)
    for l in range(L):
        xf = _ffn(xf, row(ffn1_norm[l]), *f1, fin, l, False)

        u, sm = _inproj(xf, row(mix_norm[l]), w_in_r, small_row(gdn_a_log[l]), small_row(gdn_dt_bias[l]), l)
        u3 = u.reshape(B, S, N_U)
        sm3 = sm.reshape(B, S, DH)
        ya = _gdn(u3, sm3, gdn_conv[l].astype(F32), row(gdn_out_norm[l]))
        yb = _nsa(u3, sm3, nsa_cmp_pe_k[l].astype(F32), nsa_cmp_pe_v[l].astype(F32),
                  nsa_cmp_k_w1[l], nsa_cmp_k_w2[l], nsa_cmp_v_w1[l], nsa_cmp_v_w2[l])
        oc = _hgrn(u3, lower_bounds[l].reshape(HEADS, 1, DH))
        xf = _merge(xf, u, ya.reshape(T, -1), yb.reshape(T, -1), oc.reshape(T, -1), row(hgrn_out_norm[l]),
                    *projs, l)

        xf = _ffn(xf, row(ffn2_norm[l]), *f2, fin, l, l == L - 1)
    return xf.reshape(B, S, D)
```

```python
import functools

import jax
import jax.numpy as jnp
from jax import lax
from jax.experimental import pallas as pl
from jax.experimental.pallas import tpu as pltpu

F32 = jnp.float32
BF16 = jnp.bfloat16

D_MODEL = 2048
D_FF = 5632
NORM_EPS = 1e-6
NEG_INF = -1e30
FORCE = 1e6
LOG_FLOOR = 1e-30

HEADS = 8
DH = 128
CHUNK = 64
GDN_CONV = 4
GDN_HPS = 2
NSA_GROUPS = 2
NSA_REP = HEADS // NSA_GROUPS
CMP_LEN = 32
CMP_STRIDE = 16
SLC_LEN = 64
SLC_SHIFT = 6
SLC_TOPN = 8
WIN_LEN = 512
NSA_TQ = 128
NSA_KEY_STEP = 256

OFF_GQ, OFF_GK, OFF_GV, OFF_GG = 0, 1024, 2048, 3072
OFF_NQ, OFF_NKV, OFF_SM = 4096, 5120, 6656
OFF_HQ, OFF_HF, OFF_HI, OFF_HG = 7168, 8192, 9216, 10240
OFF_MG = 11264
N_U = 17408
SM_BETA, SM_A, SM_NGATE = 0, 8, 16

VMEM_LIMIT = 56 * 1024 * 1024


def _cparams(sem):
    return pltpu.CompilerParams(dimension_semantics=sem, vmem_limit_bytes=VMEM_LIMIT)


def _sigmoid(x):
    return 1.0 / (1.0 + jnp.exp(-x))


def _silu(x):
    return x * _sigmoid(x)


def _rms(x, g):
    return x * lax.rsqrt(jnp.mean(x * x, axis=-1, keepdims=True) + NORM_EPS) * g


def _mm_nt(a, b):
    return lax.dot_general(a.astype(BF16), b.astype(BF16), (((1,), (1,)), ((), ())),
                           preferred_element_type=F32)


def _mm_tn(a, b):
    return lax.dot_general(a.astype(BF16), b.astype(BF16), (((0,), (0,)), ((), ())),
                           preferred_element_type=F32)


def _bmm(a, b):
    return jnp.einsum('cmk,ckn->cmn', a.astype(BF16), b.astype(BF16), preferred_element_type=F32)


def _bmm_nt(a, b):
    return jnp.einsum('cmk,cnk->cmn', a.astype(BF16), b.astype(BF16), preferred_element_type=F32)


def _bmm_tn(a, b):
    return jnp.einsum('ckm,ckn->cmn', a.astype(BF16), b.astype(BF16), preferred_element_type=F32)


def _split2(x):
    hi = x.astype(BF16)
    return hi, (x - hi.astype(F32)).astype(BF16)


def _mm_nt3(a, b):
    (ah, al), (bh, bl) = _split2(a), _split2(b)
    return _mm_nt(ah, bh) + (_mm_nt(ah, bl) + _mm_nt(al, bh))


def _mm3(a, b):
    (ah, al), (bh, bl) = _split2(a), _split2(b)
    d = lambda p, r: jnp.dot(p, r, preferred_element_type=F32)
    return d(ah, bh) + (d(ah, bl) + d(al, bh))


def _ffn_kernel(x_ref, g_ref, wg_ref, wu_ref, wd_ref, fg_ref, o_ref, h_ref, *, final_norm):
    f = pl.program_id(1)

    @pl.when(f == 0)
    def _():
        x = x_ref[...]
        h_ref[...] = _rms(x, g_ref[...]).astype(BF16)
        o_ref[...] = x

    h = h_ref[...]
    a = jnp.dot(h, wg_ref[...], preferred_element_type=F32)
    b = jnp.dot(h, wu_ref[...], preferred_element_type=F32)
    act = (_silu(a) * b * 0.5).astype(BF16)
    o_ref[...] += jnp.dot(act, wd_ref[...], preferred_element_type=F32)

    if final_norm:
        @pl.when(f == pl.num_programs(1) - 1)
        def _():
            o_ref[...] = _rms(o_ref[...], fg_ref[...])


def _ffn(x, g, wg, wu, wd, fg, l, final_norm, tm=1024, tf=256):
    T, D = x.shape
    F = wg.shape[2]
    return pl.pallas_call(
        functools.partial(_ffn_kernel, final_norm=final_norm),
        grid=(T // tm, F // tf),
        in_specs=[
            pl.BlockSpec((tm, D), lambda i, f: (i, 0)),
            pl.BlockSpec((1, D), lambda i, f: (0, 0)),
            pl.BlockSpec((None, D, tf), lambda i, f: (l, 0, f)),
            pl.BlockSpec((None, D, tf), lambda i, f: (l, 0, f)),
            pl.BlockSpec((None, tf, D), lambda i, f: (l, f, 0)),
            pl.BlockSpec((1, D), lambda i, f: (0, 0)),
        ],
        out_specs=pl.BlockSpec((tm, D), lambda i, f: (i, 0)),
        out_shape=jax.ShapeDtypeStruct((T, D), F32),
        scratch_shapes=[pltpu.VMEM((tm, D), BF16)],
        compiler_params=_cparams(("parallel", "arbitrary")),
        name="ffn",
    )(x, g, wg, wu, wd, fg)


def _inproj_kernel(x_ref, g_ref, w_ref, alog_ref, dtb_ref, o_ref, sm_ref, h_ref, *, tn):
    n = pl.program_id(1)

    @pl.when(n == 0)
    def _():
        h_ref[...] = _rms(x_ref[...], g_ref[...]).astype(BF16)

    acc = jnp.dot(h_ref[...], w_ref[...], preferred_element_type=F32)
    o_ref[...] = acc.astype(o_ref.dtype)

    @pl.when(n == OFF_SM // tn)
    def _():
        sm = acc[:, OFF_SM % tn:OFF_SM % tn + DH]
        lane = lax.broadcasted_iota(jnp.int32, (1, DH), 1)
        z = sm + dtb_ref[...]
        softplus = jnp.maximum(z, 0.0) + jnp.log1p(jnp.exp(-jnp.abs(z)))
        sm_ref[...] = jnp.where((lane >= SM_A) & (lane < SM_A + HEADS),
                                -jnp.exp(alog_ref[...]) * softplus, _sigmoid(sm))


def _inproj(x, g, w, a_log_row, dt_bias_row, l, tm=1024, tn=1024):
    T, D = x.shape
    N = w.shape[2]
    return pl.pallas_call(
        functools.partial(_inproj_kernel, tn=tn),
        grid=(T // tm, N // tn),
        in_specs=[
            pl.BlockSpec((tm, D), lambda i, n: (i, 0)),
            pl.BlockSpec((1, D), lambda i, n: (0, 0)),
            pl.BlockSpec((None, D, tn), lambda i, n: (l, 0, n)),
            pl.BlockSpec((1, DH), lambda i, n: (0, 0)),
            pl.BlockSpec((1, DH), lambda i, n: (0, 0)),
        ],
        out_specs=[pl.BlockSpec((tm, tn), lambda i, n: (i, n)),
                   pl.BlockSpec((tm, DH), lambda i, n: (i, 0))],
        out_shape=[jax.ShapeDtypeStruct((T, N), BF16), jax.ShapeDtypeStruct((T, DH), F32)],
        scratch_shapes=[pltpu.VMEM((tm, D), BF16)],
        compiler_params=_cparams(("parallel", "arbitrary")),
        name="in_proj",
    )(x, g, w, a_log_row, dt_bias_row)


def _gdn_kernel(q_ref, k_ref, v_ref, gate_ref, sm_ref, cq_ref, ck_ref, cv_ref, ng_ref,
                o_ref, kwt_s, bt_s, sp_s, eg_s, o0_s, qe_s):
    hp = pl.program_id(1)
    S = q_ref.shape[0]
    n = S // CHUNK
    C = CHUNK
    row8 = lax.broadcasted_iota(jnp.int32, (8, DH), 0)
    ti = lax.broadcasted_iota(jnp.int32, (1, C, C), 1)
    si = lax.broadcasted_iota(jnp.int32, (1, C, C), 2)
    lane = lax.broadcasted_iota(jnp.int32, (1, 1, DH), 2)
    causal = ti >= si
    strict = ti > si
    tril = jnp.broadcast_to(jnp.where(causal, 1.0, 0.0).astype(BF16), (n, C, C))
    eye = jnp.where(ti == si, 1.0, 0.0)

    def conv_silu(x, w):
        taps = [w[GDN_CONV - 1 - j:GDN_CONV - j] for j in range(GDN_CONV)]
        y = x * taps[0]
        for j in range(1, GDN_CONV):
            y = y + pltpu.roll(x, j, axis=0) * taps[j]
        x8 = x[0:8]
        y8 = x8 * taps[0]
        for j in range(1, GDN_CONV):
            y8 = y8 + jnp.where(row8 >= j, pltpu.roll(x8, j, axis=0), 0.0) * taps[j]
        return _silu(jnp.concatenate([y8, y[8:]], axis=0))

    def l2n(x):
        return x * lax.rsqrt(jnp.sum(x * x, axis=-1, keepdims=True) + 1e-6)

    def prepare(i):
        cols = slice(i * DH, (i + 1) * DH)
        h = hp * GDN_HPS + i
        q = l2n(conv_silu(q_ref[:, cols].astype(F32), cq_ref[:, cols])) * (DH ** -0.5)
        k = l2n(conv_silu(k_ref[:, cols].astype(F32), ck_ref[:, cols]))
        v = conv_silu(v_ref[:, cols].astype(F32), cv_ref[:, cols])
        q3, k3, v3 = (t.reshape(n, C, DH) for t in (q, k, v))

        sm3 = sm_ref[...].reshape(n, C, DH)
        beta3 = jnp.sum(jnp.where(lane == SM_BETA + h, sm3, 0.0), axis=-1, keepdims=True)
        g3 = jnp.sum(jnp.where(lane == SM_A + h, sm3, 0.0), axis=-1, keepdims=True)

        gc3 = sum(_bmm(tril, p) for p in _split2(jnp.broadcast_to(g3, (n, C, DH))))
        gc_row = jnp.swapaxes(gc3, 1, 2)[:, :C, :]
        decay = jnp.where(causal, jnp.exp(jnp.where(causal, gc3[:, :, :C] - gc_row, 0.0)), 0.0)

        kb3 = k3 * beta3
        m = jnp.where(strict, _bmm_nt(kb3, k3) * decay, 0.0)

        def lower_left(blk):
            half = blk // 2
            pair = ((ti & (blk - 1)) >= half) & ((si & (blk - 1)) < half)
            if blk < C:
                pair = pair & ((ti & -blk) == (si & -blk))
            return jnp.where(pair, m, 0.0)

        x = eye - lower_left(2)
        for blk in (4, 8, 16, 32, 64):
            x = x - _bmm(_bmm(x, lower_left(blk)), x)
        egc = jnp.exp(gc3)
        uw = _bmm(x, jnp.concatenate([v3 * beta3, kb3 * egc], axis=-1))
        attn = jnp.where(causal, _bmm_nt(q3, k3) * decay, 0.0)
        g_last = gc3[:, C - 1:C, :]
        kg = k3 * jnp.exp(g_last - gc3)
        au = _bmm(attn, uw)
        o0_s[i] = au[:, :, :DH]
        qe_s[i] = (q3 * egc - au[:, :, DH:]).astype(BF16)
        kwb = _bmm_tn(uw, kg)
        bt_s[i] = kwb[:, :DH, :]
        kwt_s[i] = kwb[:, DH:, :].astype(BF16)
        eg_s[i] = jnp.exp(g_last)

    for i in range(GDN_HPS):
        prepare(i)

    def body(c, sts):
        nxt = []
        for i, st in enumerate(sts):
            stb = st.astype(BF16)
            sp_s[i, c] = stb
            nxt.append(st * eg_s[i, c] - jnp.dot(stb, kwt_s[i, c], preferred_element_type=F32) + bt_s[i, c])
        return tuple(nxt)

    lax.fori_loop(0, n, body, tuple(jnp.zeros((DH, DH), F32) for _ in range(GDN_HPS)))

    for i in range(GDN_HPS):
        cols = slice(i * DH, (i + 1) * DH)
        o = (o0_s[i] + _bmm_nt(qe_s[i], sp_s[i])).reshape(S, DH)
        o_ref[:, cols] = (_rms(o, ng_ref[...]) * _silu(gate_ref[:, cols].astype(F32))).astype(o_ref.dtype)


def _gdn(u3, sm3, conv_w, norm_g):
    B, S, _ = u3.shape
    n = S // CHUNK
    W = GDN_HPS * DH

    def col(off):
        return pl.BlockSpec((None, S, W), lambda b, h: (b, 0, off // W + h))

    def cw(off):
        return pl.BlockSpec((GDN_CONV, W), lambda b, h: (0, off // W + h))

    return pl.pallas_call(
        _gdn_kernel,
        grid=(B, HEADS // GDN_HPS),
        in_specs=[
            col(OFF_GQ), col(OFF_GK), col(OFF_GV), col(OFF_GG),
            pl.BlockSpec((None, S, DH), lambda b, h: (b, 0, 0)),
            cw(0), cw(1024), cw(2048),
            pl.BlockSpec((1, DH), lambda b, h: (0, 0)),
        ],
        out_specs=pl.BlockSpec((None, S, W), lambda b, h: (b, 0, h)),
        out_shape=jax.ShapeDtypeStruct((B, S, HEADS * DH), BF16),
        scratch_shapes=[
            pltpu.VMEM((GDN_HPS, n, DH, DH), BF16), pltpu.VMEM((GDN_HPS, n, DH, DH), F32),
            pltpu.VMEM((GDN_HPS, n, DH, DH), BF16), pltpu.VMEM((GDN_HPS, n, 1, DH), F32),
            pltpu.VMEM((GDN_HPS, n, CHUNK, DH), F32), pltpu.VMEM((GDN_HPS, n, CHUNK, DH), BF16),
        ],
        compiler_params=_cparams(("parallel", "arbitrary")),
        name="gdn",
    )(u3, u3, u3, u3, sm3, conv_w, conv_w, conv_w, norm_g)


def _hgrn_kernel(q_ref, f_ref, i_ref, lb_ref, o_ref, kv_s, sp_s, eb_s):
    S = q_ref.shape[0]
    C = CHUNK
    n = S // C
    lb = lb_ref[...]
    q3 = _silu(q_ref[...].astype(F32)).reshape(n, C, DH)
    f = lb + (1.0 - lb) * _sigmoid(f_ref[...].astype(F32))
    logf = jnp.log(jnp.maximum(f, LOG_FLOOR))
    k3 = (1.0 - f).reshape(n, C, DH)
    v3 = i_ref[...].reshape(n, C, DH)

    ti = lax.broadcasted_iota(jnp.int32, (1, C, C), 1)
    si = lax.broadcasted_iota(jnp.int32, (1, C, C), 2)
    row = lax.broadcasted_iota(jnp.int32, (1, C, DH), 1)
    tril = jnp.broadcast_to(jnp.where(ti >= si, 1.0, 0.0).astype(BF16), (n, C, C))
    b3 = sum(_bmm(tril, p) for p in _split2(logf.reshape(n, C, DH)))
    b = b3.reshape(S, DH)

    def shifted(t, shift):
        return pltpu.roll(t.reshape(S, DH), shift, axis=0).reshape(n, C, DH)

    def mid_rows(blk):
        half = blk // 2
        if blk >= 8:
            t4 = b.reshape(S // blk, blk, DH)
            return jnp.broadcast_to(t4[:, half:half + 1, :], t4.shape).reshape(n, C, DH)
        if blk == 4:
            d = jnp.where((row & 1) == 1, shifted(b3, 1), b3)
            return jnp.where((row & 3) < 2, shifted(d, S - 2), d)
        return jnp.where((row & 1) == 0, shifted(b3, S - 1), b3)

    a = jnp.where(ti == si, jnp.sum(q3 * k3, axis=-1, keepdims=True), 0.0)
    for blk in (64, 32, 16, 8, 4, 2):
        half = blk // 2
        upper = (row & (blk - 1)) >= half
        x = jnp.where(upper, q3, k3) * jnp.exp(-jnp.abs(b3 - mid_rows(blk)))
        pair = ((ti & (blk - 1)) >= half) & ((si & (blk - 1)) < half)
        if blk < C:
            pair = pair & ((ti & -blk) == (si & -blk))
        a = jnp.where(pair, _bmm_nt(x, x), a)

    o_intra = _bmm(a, v3)
    b_last = b3[:, C - 1:C, :]
    kv_s[...] = _bmm_tn(v3, k3 * jnp.exp(b_last - b3))
    eb_s[...] = jnp.exp(b_last)

    def body(c, st):
        sp_s[c] = st.astype(BF16)
        return st * eb_s[c] + kv_s[c]

    lax.fori_loop(0, n, body, jnp.zeros((DH, DH), F32))
    o_ref[...] = (o_intra + _bmm_nt(q3 * jnp.exp(b3), sp_s[...])).reshape(S, DH).astype(o_ref.dtype)


def _hgrn(u3, lb_b):
    B, S, _ = u3.shape
    n = S // CHUNK

    def col(off):
        return pl.BlockSpec((None, S, DH), lambda b, h: (b, 0, off // DH + h))

    return pl.pallas_call(
        _hgrn_kernel,
        grid=(B, HEADS),
        in_specs=[col(OFF_HQ), col(OFF_HF), col(OFF_HI),
                  pl.BlockSpec((None, 1, DH), lambda b, h: (h, 0, 0))],
        out_specs=pl.BlockSpec((None, S, DH), lambda b, h: (b, 0, h)),
        out_shape=jax.ShapeDtypeStruct((B, S, HEADS * DH), BF16),
        scratch_shapes=[
            pltpu.VMEM((n, DH, DH), F32), pltpu.VMEM((n, DH, DH), BF16), pltpu.VMEM((n, 1, DH), F32),
        ],
        compiler_params=_cparams(("parallel", "arbitrary")),
        name="hgrn2",
    )(u3, u3, u3, lb_b)


def _nsa_kernel(q_ref, kc_ref, vc_ref, ks_ref, vs_ref, kw_ref, vw_ref, sm_ref,
                pek_ref, pev_ref, wk1_ref, wk2_ref, wv1_ref, wv2_ref,
                o_ref, x_s, kc_s, vc_s, kw_s, vt_s, vwt_s, oslc_s):
    g = pl.program_id(1)
    qi = pl.program_id(2)
    S = ks_ref.shape[0]
    TQ = NSA_TQ
    R = NSA_REP
    NC = S // CMP_STRIDE
    NB = S // SLC_LEN

    @pl.when(qi == 0)
    def _():
        def compress(x_ref, pe_ref, w1_ref, w2_ref):
            x_s[...] = x_ref[...].astype(F32)
            a = jnp.zeros((NC, DH), F32)
            b = jnp.zeros((NC, DH), F32)
            for j in range(CMP_STRIDE):
                xj = x_s[pl.ds(j, NC, stride=CMP_STRIDE), :]
                jb = CMP_STRIDE + j
                a = a + _mm3(xj + pe_ref[j:j + 1, :], w1_ref[j * DH:(j + 1) * DH, :])
                b = b + _mm3(xj + pe_ref[jb:jb + 1, :], w1_ref[jb * DH:(jb + 1) * DH, :])
            pre = a + pltpu.roll(b, NC - 1, axis=0)
            return _mm3(_silu(pre), w2_ref[...])

        kc_s[...] = compress(kc_ref, pek_ref, wk1_ref, wk2_ref)
        vc_s[...] = compress(vc_ref, pev_ref, wv1_ref, wv2_ref).astype(BF16)
        kw_s[0:WIN_LEN, :] = jnp.zeros((WIN_LEN, DH), BF16)
        kw_s[WIN_LEN:, :] = kw_ref[...]
        vt_s[...] = vs_ref[...].astype(F32).T.astype(BF16)
        vw_t = vw_ref[...].astype(F32).T.astype(BF16)
        for j in range(WIN_LEN // TQ):
            vwt_s[j] = jnp.zeros((DH, TQ), BF16)
        for j in range(S // TQ):
            vwt_s[WIN_LEN // TQ + j] = vw_t[:, j * TQ:(j + 1) * TQ]

    t0 = qi * TQ
    qb = q_ref[...].astype(F32) * (DH ** -0.5)
    q4 = jnp.concatenate([qb[:, r * DH:(r + 1) * DH] for r in range(R)], axis=0)
    q4b = q4.astype(BF16)

    def iota(shape, dim):
        return lax.broadcasted_iota(jnp.int32, shape, dim)

    sc = _mm_nt3(q4, kc_s[...])
    n_id = iota((R * TQ, NC), 1)
    t_c = t0 + (iota((R * TQ, NC), 0) & (TQ - 1))
    ok_c = (n_id * CMP_STRIDE + (CMP_LEN - 1) <= t_c) & (n_id < NC - 1)
    m_c = jnp.max(jnp.where(ok_c, sc, NEG_INF), axis=-1, keepdims=True)
    e_c = jnp.where(ok_c, jnp.exp(sc - m_c), 0.0)
    den_c = jnp.sum(e_c, axis=-1, keepdims=True)
    p_c = e_c / jnp.where(den_c > 0.0, den_c, 1.0)
    o_cmp = jnp.dot(p_c.astype(BF16), vc_s[...], preferred_element_type=F32)

    p_sum = p_c[0:TQ]
    for r in range(1, R):
        p_sum = p_sum + p_c[r * TQ:(r + 1) * TQ]
    js = iota((NB, NC), 0) * SLC_LEN
    cn = iota((NB, NC), 1) * CMP_STRIDE
    overlap_t = jnp.where((cn < js + SLC_LEN) & (cn + CMP_LEN > js), 1.0, 0.0).astype(BF16)
    imp = sum(_mm_nt(overlap_t, p) for p in _split2(p_sum))
    blk = iota((NB, TQ), 0)
    cur = (t0 + iota((NB, TQ), 1)) >> SLC_SHIFT
    forced = (blk == 0) | (blk == cur) | (blk == cur - 1)
    work = jnp.where(blk > cur, -FORCE, jnp.where(forced, FORCE, imp))
    blkf = blk.astype(F32)
    sel_bias = jnp.full((NB, TQ), NEG_INF, F32)
    for _ in range(SLC_TOPN):
        mx = jnp.max(work, axis=0, keepdims=True)
        idx = jnp.min(jnp.where(work == mx, blkf, float(NB)), axis=0, keepdims=True)
        hit = blkf == idx
        sel_bias = jnp.where(hit, 0.0, sel_bias)
        work = jnp.where(hit, -jnp.inf, work)
    sel_bias = sel_bias.astype(BF16)

    def selected(ne):
        expand = jnp.where((iota((NB, ne), 1) >> SLC_SHIFT) == iota((NB, ne), 0), 1.0, 0.0).astype(BF16)
        bias = _mm_tn(sel_bias, expand)
        bias = jnp.where(iota((TQ, ne), 1) <= t0 + iota((TQ, ne), 0), bias, NEG_INF)
        oslc_s[...] = attend(ks_ref[0:ne, :], vt_s[:, 0:ne], bias)

    def attend(k, v_t, bias):
        nk = k.shape[0]
        s = _mm_nt(q4b, k).reshape(R, TQ, nk) + bias[None]
        e = jnp.exp(s - jnp.max(s, axis=-1, keepdims=True))
        den = jnp.sum(e, axis=-1, keepdims=True)
        o_t = _mm_nt(v_t, e.reshape(R * TQ, nk))
        return jnp.concatenate([o_t[:, r * TQ:(r + 1) * TQ].T / den[r] for r in range(R)], axis=0)

    for vi in range(S // NSA_KEY_STEP):
        pl.when((t0 >= vi * NSA_KEY_STEP) & (t0 < (vi + 1) * NSA_KEY_STEP))(
            functools.partial(selected, (vi + 1) * NSA_KEY_STEP))

    span = WIN_LEN + TQ
    kw = kw_s[pl.ds(pl.multiple_of(t0, TQ), span), :]
    vw_t = jnp.concatenate([vwt_s[qi + j] for j in range(span // TQ)], axis=1)
    rel = iota((TQ, span), 1) - WIN_LEN - iota((TQ, span), 0)
    ok_w = (rel <= 0) & (rel > -WIN_LEN) & (t0 - WIN_LEN + iota((TQ, span), 1) >= 0)
    o_win = attend(kw, vw_t, jnp.where(ok_w, 0.0, NEG_INF))

    sg = sm_ref[...]
    sl = iota((1, DH), 1)
    for r in range(R):
        rows = slice(r * TQ, (r + 1) * TQ)
        base = SM_NGATE + (g * R + r) * 3
        gate = [jnp.sum(jnp.where(sl == base + i, sg, 0.0), axis=-1, keepdims=True) for i in range(3)]
        o_ref[:, r * DH:(r + 1) * DH] = (gate[0] * o_cmp[rows] + gate[1] * oslc_s[rows, :]
                                         + gate[2] * o_win[rows]).astype(o_ref.dtype)


def _nsa(u3, sm3, pe_k, pe_v, wk1, wk2, wv1, wv2):
    B, S, _ = u3.shape
    G, R, TQ = NSA_GROUPS, NSA_REP, NSA_TQ
    NC = S // CMP_STRIDE

    def kv(i):
        return pl.BlockSpec((None, S, DH), lambda b, g, t: (b, 0, OFF_NKV // DH + i * G + g))

    def full(shape):
        return pl.BlockSpec(shape, lambda b, g, t: (0,) * len(shape))

    return pl.pallas_call(
        _nsa_kernel,
        grid=(B, G, S // TQ),
        in_specs=[
            pl.BlockSpec((None, TQ, R * DH), lambda b, g, t: (b, t, OFF_NQ // (R * DH) + g)),
            kv(0), kv(1), kv(2), kv(3), kv(4), kv(5),
            pl.BlockSpec((None, TQ, DH), lambda b, g, t: (b, t, 0)),
            full((CMP_LEN, DH)), full((CMP_LEN, DH)),
            full((CMP_LEN * DH, DH)), full((DH, DH)), full((CMP_LEN * DH, DH)), full((DH, DH)),
        ],
        out_specs=pl.BlockSpec((None, TQ, R * DH), lambda b, g, t: (b, t, g)),
        out_shape=jax.ShapeDtypeStruct((B, S, HEADS * DH), BF16),
        scratch_shapes=[
            pltpu.VMEM((S, DH), F32),
            pltpu.VMEM((NC, DH), F32), pltpu.VMEM((NC, DH), BF16),
            pltpu.VMEM((S + WIN_LEN, DH), BF16), pltpu.VMEM((DH, S), BF16),
            pltpu.VMEM(((S + WIN_LEN) // TQ, DH, TQ), BF16),
            pltpu.VMEM((R * TQ, DH), F32),
        ],
        compiler_params=_cparams(("parallel", "parallel", "arbitrary")),
        name="nsa",
    )(u3, u3, u3, u3, u3, u3, u3, sm3, pe_k, pe_v, wk1, wk2, wv1, wv2)


def _merge_kernel(x_ref, ya_ref, yb_ref, oc_ref, hg_ref, hn_ref, ga_ref, gb_ref, gc_ref,
                  pa_ref, pb_ref, pc_ref, wo_ref, o_ref, yc_s, mg_s, *, nk):
    c = pl.program_id(1)
    tk = o_ref.shape[1]

    @pl.when(c == 0)
    def _():
        yc_s[...] = (_rms(oc_ref[...].astype(F32), hn_ref[...]) * _silu(hg_ref[...].astype(F32))).astype(BF16)

    @pl.when(c < nk)
    def _():
        m = (_sigmoid(ga_ref[...].astype(F32)) * jnp.dot(ya_ref[...], pa_ref[...], preferred_element_type=F32)
             + _sigmoid(gb_ref[...].astype(F32)) * jnp.dot(yb_ref[...], pb_ref[...], preferred_element_type=F32)
             + _sigmoid(gc_ref[...].astype(F32)) * jnp.dot(yc_s[...], pc_ref[...], preferred_element_type=F32))
        mg_s[c] = m.astype(BF16)

    @pl.when(c >= nk)
    def _():
        acc = x_ref[...]
        for j in range(nk):
            acc = acc + jnp.dot(mg_s[j], wo_ref[j * tk:(j + 1) * tk, :], preferred_element_type=F32)
        o_ref[...] = acc


def _merge(x, u, ya, yb, oc, hnorm, pa, pb, pc, wo, l, tm=1024, tk=512):
    T, D = x.shape
    V = ya.shape[1]
    nk = D // tk
    first = lambda c: jnp.minimum(c, nk - 1)
    second = lambda c: jnp.maximum(c - nk, 0)

    def gate(i):
        return pl.BlockSpec((tm, tk), lambda m, c: (m, (OFF_MG + i * D) // tk + first(c)))

    rows = lambda w: pl.BlockSpec((tm, w), lambda m, c: (m, 0))
    proj = pl.BlockSpec((None, V, tk), lambda m, c: (l, 0, first(c)))
    return pl.pallas_call(
        functools.partial(_merge_kernel, nk=nk),
        grid=(T // tm, 2 * nk),
        in_specs=[
            pl.BlockSpec((tm, tk), lambda m, c: (m, second(c))),
            rows(V), rows(V), rows(V),
            pl.BlockSpec((tm, V), lambda m, c: (m, OFF_HG // V)),
            pl.BlockSpec((1, V), lambda m, c: (0, 0)),
            gate(0), gate(1), gate(2),
            proj, proj, proj,
            pl.BlockSpec((None, D, tk), lambda m, c: (l, 0, second(c))),
        ],
        out_specs=pl.BlockSpec((tm, tk), lambda m, c: (m, second(c))),
        out_shape=jax.ShapeDtypeStruct((T, D), F32),
        scratch_shapes=[pltpu.VMEM((tm, V), BF16), pltpu.VMEM((nk, tm, tk), BF16)],
        compiler_params=_cparams(("parallel", "arbitrary")),
        name="merge",
    )(x, ya, yb, oc, u, hnorm, u, u, u, pa, pb, pc, wo)


SRC_QKV = (0, 3072)
SRC_BA = (3072, 3088)
SRC_MID = (3088, 6672)
SRC_NGATE = (6672, 6696)
SRC_TAIL = (6696, 16936)


def _relayout_kernel(w_ref, o_ref):
    rows = w_ref.shape[0]
    cast = lambda rng: w_ref[:, rng[0]:rng[1]].astype(BF16)
    o_ref[:, OFF_GQ:OFF_GG] = cast(SRC_QKV)
    o_ref[:, OFF_GG:OFF_SM] = cast(SRC_MID)
    pad = jnp.zeros((rows, OFF_HQ - OFF_SM - (SM_NGATE + 3 * HEADS)), BF16)
    o_ref[:, OFF_SM:OFF_HQ] = jnp.concatenate([cast(SRC_BA), cast(SRC_NGATE), pad], axis=1)
    o_ref[:, OFF_HQ:N_U] = cast(SRC_TAIL)


def _relayout_w_in(w, rb=64):
    L, D, n_in = w.shape
    return pl.pallas_call(
        _relayout_kernel,
        grid=(L, D // rb),
        in_specs=[pl.BlockSpec((None, rb, n_in), lambda l, r: (l, r, 0))],
        out_specs=pl.BlockSpec((None, rb, N_U), lambda l, r: (l, r, 0)),
        out_shape=jax.ShapeDtypeStruct((L, D, N_U), BF16),
        compiler_params=_cparams(("parallel", "parallel")),
        name="relayout_w_in",
    )(w)


def kernel(x, ffn1_norm, ffn1_w_gate, ffn1_w_up, ffn1_w_down, mix_norm, w_in, gdn_conv, gdn_a_log, gdn_dt_bias, gdn_out_norm, nsa_cmp_pe_k, nsa_cmp_pe_v, nsa_cmp_k_w1, nsa_cmp_k_w2, nsa_cmp_v_w1, nsa_cmp_v_w2, hgrn_lb_logits, hgrn_out_norm, w_proj_a, w_proj_b, w_proj_c, w_out, ffn2_norm, ffn2_w_gate, ffn2_w_up, ffn2_w_down, final_norm):
    B, S, D = x.shape
    L = w_in.shape[0]
    T = B * S
    row = lambda v: v.reshape(1, -1).astype(F32)
    bf = lambda w: w.astype(BF16)
    small_row = lambda v: jnp.zeros((1, DH), F32).at[0, SM_A:SM_A + HEADS].set(v.astype(F32))

    lb_p = jax.nn.softmax(hgrn_lb_logits.astype(F32), axis=0)
    lower_bounds = jnp.cumsum(lb_p, axis=0) - lb_p[0]
    fin = row(final_norm)
    f1 = (bf(ffn1_w_gate), bf(ffn1_w_up), bf(ffn1_w_down))
    f2 = (bf(ffn2_w_gate), bf(ffn2_w_up), bf(ffn2_w_down))
    w_in_r = _relayout_w_in(w_in.astype(F32))
    projs = (bf(w_proj_a), bf(w_proj_b), bf(w_proj_c), bf(w_out))

    xf = x.reshape(T, D)
    for l in range(L):
        xf = _ffn(xf, row(ffn1_norm[l]), *f1, fin, l, False)

        u, sm = _inproj(xf, row(mix_norm[l]), w_in_r, small_row(gdn_a_log[l]), small_row(gdn_dt_bias[l]), l)
        u3 = u.reshape(B, S, N_U)
        sm3 = sm.reshape(B, S, DH)
        ya = _gdn(u3, sm3, gdn_conv[l].astype(F32), row(gdn_out_norm[l]))
        yb = _nsa(u3, sm3, nsa_cmp_pe_k[l].astype(F32), nsa_cmp_pe_v[l].astype(F32),
                  nsa_cmp_k_w1[l], nsa_cmp_k_w2[l], nsa_cmp_v_w1[l], nsa_cmp_v_w2[l])
        oc = _hgrn(u3, lower_bounds[l].reshape(HEADS, 1, DH))
        xf = _merge(xf, u, ya.reshape(T, -1), yb.reshape(T, -1), oc.reshape(T, -1), row(hgrn_out_norm[l]),
                    *projs, l)

        xf = _ffn(xf, row(ffn2_norm[l]), *f2, fin, l, l == L - 1)
    return xf.reshape(B, S, D)
```

```python
import functools

import jax
import jax.numpy as jnp
from jax import lax
from jax.experimental import pallas as pl
from jax.experimental.pallas import tpu as pltpu

F32 = jnp.float32
BF16 = jnp.bfloat16

D_MODEL = 2048
D_FF = 5632
NORM_EPS = 1e-6
NEG_INF = -1e30
FORCE = 1e6
LOG_FLOOR = 1e-30

HEADS = 8
DH = 128
CHUNK = 64
GDN_CONV = 4
GDN_HPS = 2
NSA_GROUPS = 2
NSA_REP = HEADS // NSA_GROUPS
CMP_LEN = 32
CMP_STRIDE = 16
SLC_LEN = 64
SLC_SHIFT = 6
SLC_TOPN = 8
WIN_LEN = 512
NSA_TQ = 128
NSA_KEY_STEP = 256

OFF_GQ, OFF_GK, OFF_GV, OFF_GG = 0, 1024, 2048, 3072
OFF_NQ, OFF_NKV, OFF_SM = 4096, 5120, 6656
OFF_HQ, OFF_HF, OFF_HI, OFF_HG = 7168, 8192, 9216, 10240
OFF_MG = 11264
N_U = 17408
SM_BETA, SM_A, SM_NGATE = 0, 8, 16

VMEM_LIMIT = 56 * 1024 * 1024


def _cparams(sem):
    return pltpu.CompilerParams(dimension_semantics=sem, vmem_limit_bytes=VMEM_LIMIT)


def _sigmoid(x):
    return 1.0 / (1.0 + jnp.exp(-x))


def _silu(x):
    return x * _sigmoid(x)


def _rms(x, g):
    return x * lax.rsqrt(jnp.mean(x * x, axis=-1, keepdims=True) + NORM_EPS) * g


def _mm_nt(a, b):
    return lax.dot_general(a.astype(BF16), b.astype(BF16), (((1,), (1,)), ((), ())),
                           preferred_element_type=F32)


def _mm_tn(a, b):
    return lax.dot_general(a.astype(BF16), b.astype(BF16), (((0,), (0,)), ((), ())),
                           preferred_element_type=F32)


def _bmm(a, b):
    return jnp.einsum('cmk,ckn->cmn', a.astype(BF16), b.astype(BF16), preferred_element_type=F32)


def _bmm_nt(a, b):
    return jnp.einsum('cmk,cnk->cmn', a.astype(BF16), b.astype(BF16), preferred_element_type=F32)


def _bmm_tn(a, b):
    return jnp.einsum('ckm,ckn->cmn', a.astype(BF16), b.astype(BF16), preferred_element_type=F32)


def _split2(x):
    hi = x.astype(BF16)
    return hi, (x - hi.astype(F32)).astype(BF16)


def _mm_nt3(a, b):
    (ah, al), (bh, bl) = _split2(a), _split2(b)
    return _mm_nt(ah, bh) + (_mm_nt(ah, bl) + _mm_nt(al, bh))


def _mm3(a, b):
    (ah, al), (bh, bl) = _split2(a), _split2(b)
    d = lambda p, r: jnp.dot(p, r, preferred_element_type=F32)
    return d(ah, bh) + (d(ah, bl) + d(al, bh))


def _ffn_kernel(x_ref, g_ref, wg_ref, wu_ref, wd_ref, fg_ref, o_ref, h_ref, *, final_norm):
    f = pl.program_id(1)

    @pl.when(f == 0)
    def _():
        x = x_ref[...]
        h_ref[...] = _rms(x, g_ref[...]).astype(BF16)
        o_ref[...] = x

    h = h_ref[...]
    a = jnp.dot(h, wg_ref[...].astype(BF16), preferred_element_type=F32)
    b = jnp.dot(h, wu_ref[...].astype(BF16), preferred_element_type=F32)
    act = (_silu(a) * b * 0.5).astype(BF16)
    o_ref[...] += jnp.dot(act, wd_ref[...].astype(BF16), preferred_element_type=F32)

    if final_norm:
        @pl.when(f == pl.num_programs(1) - 1)
        def _():
            o_ref[...] = _rms(o_ref[...], fg_ref[...])


def _ffn(x, g, wg, wu, wd, fg, l, final_norm, tm=1024, tf=256):
    T, D = x.shape
    F = wg.shape[2]
    return pl.pallas_call(
        functools.partial(_ffn_kernel, final_norm=final_norm),
        grid=(T // tm, F // tf),
        in_specs=[
            pl.BlockSpec((tm, D), lambda i, f: (i, 0)),
            pl.BlockSpec((1, D), lambda i, f: (0, 0)),
            pl.BlockSpec((None, D, tf), lambda i, f: (l, 0, f)),
            pl.BlockSpec((None, D, tf), lambda i, f: (l, 0, f)),
            pl.BlockSpec((None, tf, D), lambda i, f: (l, f, 0)),
            pl.BlockSpec((1, D), lambda i, f: (0, 0)),
        ],
        out_specs=pl.BlockSpec((tm, D), lambda i, f: (i, 0)),
        out_shape=jax.ShapeDtypeStruct((T, D), F32),
        scratch_shapes=[pltpu.VMEM((tm, D), BF16)],
        compiler_params=_cparams(("parallel", "arbitrary")),
        name="ffn",
    )(x, g, wg, wu, wd, fg)


def _inproj_kernel(x_ref, g_ref, w_ref, alog_ref, dtb_ref, o_ref, sm_ref, h_ref, *, tn):
    n = pl.program_id(1)

    @pl.when(n == 0)
    def _():
        h_ref[...] = _rms(x_ref[...], g_ref[...]).astype(BF16)

    acc = jnp.dot(h_ref[...], w_ref[...], preferred_element_type=F32)
    o_ref[...] = acc.astype(o_ref.dtype)

    @pl.when(n == OFF_SM // tn)
    def _():
        sm = acc[:, OFF_SM % tn:OFF_SM % tn + DH]
        lane = lax.broadcasted_iota(jnp.int32, (1, DH), 1)
        z = sm + dtb_ref[...]
        softplus = jnp.maximum(z, 0.0) + jnp.log1p(jnp.exp(-jnp.abs(z)))
        sm_ref[...] = jnp.where((lane >= SM_A) & (lane < SM_A + HEADS),
                                -jnp.exp(alog_ref[...]) * softplus, _sigmoid(sm))


def _inproj(x, g, w, a_log_row, dt_bias_row, l, tm=1024, tn=1024):
    T, D = x.shape
    N = w.shape[2]
    return pl.pallas_call(
        functools.partial(_inproj_kernel, tn=tn),
        grid=(T // tm, N // tn),
        in_specs=[
            pl.BlockSpec((tm, D), lambda i, n: (i, 0)),
            pl.BlockSpec((1, D), lambda i, n: (0, 0)),
            pl.BlockSpec((None, D, tn), lambda i, n: (l, 0, n)),
            pl.BlockSpec((1, DH), lambda i, n: (0, 0)),
            pl.BlockSpec((1, DH), lambda i, n: (0, 0)),
        ],
        out_specs=[pl.BlockSpec((tm, tn), lambda i, n: (i, n)),
                   pl.BlockSpec((tm, DH), lambda i, n: (i, 0))],
        out_shape=[jax.ShapeDtypeStruct((T, N), BF16), jax.ShapeDtypeStruct((T, DH), F32)],
        scratch_shapes=[pltpu.VMEM((tm, D), BF16)],
        compiler_params=_cparams(("parallel", "arbitrary")),
        name="in_proj",
    )(x, g, w, a_log_row, dt_bias_row)


def _gdn_kernel(q_ref, k_ref, v_ref, gate_ref, sm_ref, cq_ref, ck_ref, cv_ref, ng_ref,
                o_ref, kwt_s, bt_s, sp_s, eg_s, o0_s, qe_s):
    hp = pl.program_id(1)
    S = q_ref.shape[0]
    n = S // CHUNK
    C = CHUNK
    row8 = lax.broadcasted_iota(jnp.int32, (8, DH), 0)
    ti = lax.broadcasted_iota(jnp.int32, (1, C, C), 1)
    si = lax.broadcasted_iota(jnp.int32, (1, C, C), 2)
    lane = lax.broadcasted_iota(jnp.int32, (1, 1, DH), 2)
    causal = ti >= si
    strict = ti > si
    tril = jnp.broadcast_to(jnp.where(causal, 1.0, 0.0).astype(BF16), (n, C, C))
    eye = jnp.where(ti == si, 1.0, 0.0)

    def conv_silu(x, w):
        taps = [w[GDN_CONV - 1 - j:GDN_CONV - j] for j in range(GDN_CONV)]
        y = x * taps[0]
        for j in range(1, GDN_CONV):
            y = y + pltpu.roll(x, j, axis=0) * taps[j]
        x8 = x[0:8]
        y8 = x8 * taps[0]
        for j in range(1, GDN_CONV):
            y8 = y8 + jnp.where(row8 >= j, pltpu.roll(x8, j, axis=0), 0.0) * taps[j]
        return _silu(jnp.concatenate([y8, y[8:]], axis=0))

    def l2n(x):
        return x * lax.rsqrt(jnp.sum(x * x, axis=-1, keepdims=True) + 1e-6)

    def prepare(i):
        cols = slice(i * DH, (i + 1) * DH)
        h = hp * GDN_HPS + i
        q = l2n(conv_silu(q_ref[:, cols].astype(F32), cq_ref[:, cols])) * (DH ** -0.5)
        k = l2n(conv_silu(k_ref[:, cols].astype(F32), ck_ref[:, cols]))
        v = conv_silu(v_ref[:, cols].astype(F32), cv_ref[:, cols])
        q3, k3, v3 = (t.reshape(n, C, DH) for t in (q, k, v))

        sm3 = sm_ref[...].reshape(n, C, DH)
        beta3 = jnp.sum(jnp.where(lane == SM_BETA + h, sm3, 0.0), axis=-1, keepdims=True)
        g3 = jnp.sum(jnp.where(lane == SM_A + h, sm3, 0.0), axis=-1, keepdims=True)

        gc3 = sum(_bmm(tril, p) for p in _split2(jnp.broadcast_to(g3, (n, C, DH))))
        gc_row = jnp.swapaxes(gc3, 1, 2)[:, :C, :]
        decay = jnp.where(causal, jnp.exp(jnp.where(causal, gc3[:, :, :C] - gc_row, 0.0)), 0.0)

        kb3 = k3 * beta3
        m = jnp.where(strict, _bmm_nt(kb3, k3) * decay, 0.0)

        def lower_left(blk):
            half = blk // 2
            pair = ((ti & (blk - 1)) >= half) & ((si & (blk - 1)) < half)
            if blk < C:
                pair = pair & ((ti & -blk) == (si & -blk))
            return jnp.where(pair, m, 0.0)

        x = eye - lower_left(2)
        for blk in (4, 8, 16, 32, 64):
            x = x - _bmm(_bmm(x, lower_left(blk)), x)
        egc = jnp.exp(gc3)
        uw = _bmm(x, jnp.concatenate([v3 * beta3, kb3 * egc], axis=-1))
        attn = jnp.where(causal, _bmm_nt(q3, k3) * decay, 0.0)
        g_last = gc3[:, C - 1:C, :]
        kg = k3 * jnp.exp(g_last - gc3)
        au = _bmm(attn, uw)
        o0_s[i] = au[:, :, :DH]
        qe_s[i] = (q3 * egc - au[:, :, DH:]).astype(BF16)
        kwb = _bmm_tn(uw, kg)
        bt_s[i] = kwb[:, :DH, :]
        kwt_s[i] = kwb[:, DH:, :].astype(BF16)
        eg_s[i] = jnp.exp(g_last)

    for i in range(GDN_HPS):
        prepare(i)

    def body(c, sts):
        nxt = []
        for i, st in enumerate(sts):
            stb = st.astype(BF16)
            sp_s[i, c] = stb
            nxt.append(st * eg_s[i, c] - jnp.dot(stb, kwt_s[i, c], preferred_element_type=F32) + bt_s[i, c])
        return tuple(nxt)

    lax.fori_loop(0, n, body, tuple(jnp.zeros((DH, DH), F32) for _ in range(GDN_HPS)))

    for i in range(GDN_HPS):
        cols = slice(i * DH, (i + 1) * DH)
        o = (o0_s[i] + _bmm_nt(qe_s[i], sp_s[i])).reshape(S, DH)
        o_ref[:, cols] = (_rms(o, ng_ref[...]) * _silu(gate_ref[:, cols].astype(F32))).astype(o_ref.dtype)


def _gdn(u3, sm3, conv_w, norm_g):
    B, S, _ = u3.shape
    n = S // CHUNK
    W = GDN_HPS * DH

    def col(off):
        return pl.BlockSpec((None, S, W), lambda b, h: (b, 0, off // W + h))

    def cw(off):
        return pl.BlockSpec((GDN_CONV, W), lambda b, h: (0, off // W + h))

    return pl.pallas_call(
        _gdn_kernel,
        grid=(B, HEADS // GDN_HPS),
        in_specs=[
            col(OFF_GQ), col(OFF_GK), col(OFF_GV), col(OFF_GG),
            pl.BlockSpec((None, S, DH), lambda b, h: (b, 0, 0)),
            cw(0), cw(1024), cw(2048),
            pl.BlockSpec((1, DH), lambda b, h: (0, 0)),
        ],
        out_specs=pl.BlockSpec((None, S, W), lambda b, h: (b, 0, h)),
        out_shape=jax.ShapeDtypeStruct((B, S, HEADS * DH), BF16),
        scratch_shapes=[
            pltpu.VMEM((GDN_HPS, n, DH, DH), BF16), pltpu.VMEM((GDN_HPS, n, DH, DH), F32),
            pltpu.VMEM((GDN_HPS, n, DH, DH), BF16), pltpu.VMEM((GDN_HPS, n, 1, DH), F32),
            pltpu.VMEM((GDN_HPS, n, CHUNK, DH), F32), pltpu.VMEM((GDN_HPS, n, CHUNK, DH), BF16),
        ],
        compiler_params=_cparams(("parallel", "arbitrary")),
        name="gdn",
    )(u3, u3, u3, u3, sm3, conv_w, conv_w, conv_w, norm_g)


def _hgrn_kernel(q_ref, f_ref, i_ref, lb_ref, o_ref, kv_s, sp_s, eb_s):
    S = q_ref.shape[0]
    C = CHUNK
    n = S // C
    lb = lb_ref[...]
    q3 = _silu(q_ref[...].astype(F32)).reshape(n, C, DH)
    f = lb + (1.0 - lb) * _sigmoid(f_ref[...].astype(F32))
    logf = jnp.log(jnp.maximum(f, LOG_FLOOR))
    k3 = (1.0 - f).reshape(n, C, DH)
    v3 = i_ref[...].reshape(n, C, DH)

    ti = lax.broadcasted_iota(jnp.int32, (1, C, C), 1)
    si = lax.broadcasted_iota(jnp.int32, (1, C, C), 2)
    row = lax.broadcasted_iota(jnp.int32, (1, C, DH), 1)
    tril = jnp.broadcast_to(jnp.where(ti >= si, 1.0, 0.0).astype(BF16), (n, C, C))
    b3 = sum(_bmm(tril, p) for p in _split2(logf.reshape(n, C, DH)))
    b = b3.reshape(S, DH)

    def shifted(t, shift):
        return pltpu.roll(t.reshape(S, DH), shift, axis=0).reshape(n, C, DH)

    def mid_rows(blk):
        half = blk // 2
        if blk >= 8:
            t4 = b.reshape(S // blk, blk, DH)
            return jnp.broadcast_to(t4[:, half:half + 1, :], t4.shape).reshape(n, C, DH)
        if blk == 4:
            d = jnp.where((row & 1) == 1, shifted(b3, 1), b3)
            return jnp.where((row & 3) < 2, shifted(d, S - 2), d)
        return jnp.where((row & 1) == 0, shifted(b3, S - 1), b3)

    a = jnp.where(ti == si, jnp.sum(q3 * k3, axis=-1, keepdims=True), 0.0)
    for blk in (64, 32, 16, 8, 4, 2):
        half = blk // 2
        upper = (row & (blk - 1)) >= half
        x = jnp.where(upper, q3, k3) * jnp.exp(-jnp.abs(b3 - mid_rows(blk)))
        pair = ((ti & (blk - 1)) >= half) & ((si & (blk - 1)) < half)
        if blk < C:
            pair = pair & ((ti & -blk) == (si & -blk))
        a = jnp.where(pair, _bmm_nt(x, x), a)

    o_intra = _bmm(a, v3)
    b_last = b3[:, C - 1:C, :]
    kv_s[...] = _bmm_tn(v3, k3 * jnp.exp(b_last - b3))
    eb_s[...] = jnp.exp(b_last)

    def body(c, st):
        sp_s[c] = st.astype(BF16)
        return st * eb_s[c] + kv_s[c]

    lax.fori_loop(0, n, body, jnp.zeros((DH, DH), F32))
    o_ref[...] = (o_intra + _bmm_nt(q3 * jnp.exp(b3), sp_s[...])).reshape(S, DH).astype(o_ref.dtype)


def _hgrn(u3, lb_b):
    B, S, _ = u3.shape
    n = S // CHUNK

    def col(off):
        return pl.BlockSpec((None, S, DH), lambda b, h: (b, 0, off // DH + h))

    return pl.pallas_call(
        _hgrn_kernel,
        grid=(B, HEADS),
        in_specs=[col(OFF_HQ), col(OFF_HF), col(OFF_HI),
                  pl.BlockSpec((None, 1, DH), lambda b, h: (h, 0, 0))],
        out_specs=pl.BlockSpec((None, S, DH), lambda b, h: (b, 0, h)),
        out_shape=jax.ShapeDtypeStruct((B, S, HEADS * DH), BF16),
        scratch_shapes=[
            pltpu.VMEM((n, DH, DH), F32), pltpu.VMEM((n, DH, DH), BF16), pltpu.VMEM((n, 1, DH), F32),
        ],
        compiler_params=_cparams(("parallel", "arbitrary")),
        name="hgrn2",
    )(u3, u3, u3, lb_b)


def _nsa_kernel(q_ref, kc_ref, vc_ref, ks_ref, vs_ref, kw_ref, vw_ref, sm_ref,
                pek_ref, pev_ref, wk1_ref, wk2_ref, wv1_ref, wv2_ref,
                o_ref, x_s, kc_s, vc_s, kw_s, vt_s, vwt_s, oslc_s):
    g = pl.program_id(1)
    qi = pl.program_id(2)
    S = ks_ref.shape[0]
    TQ = NSA_TQ
    R = NSA_REP
    NC = S // CMP_STRIDE
    NB = S // SLC_LEN

    @pl.when(qi == 0)
    def _():
        def compress(x_ref, pe_ref, w1_ref, w2_ref):
            x_s[...] = x_ref[...].astype(F32)
            a = jnp.zeros((NC, DH), F32)
            b = jnp.zeros((NC, DH), F32)
            for j in range(CMP_STRIDE):
                xj = x_s[pl.ds(j, NC, stride=CMP_STRIDE), :]
                jb = CMP_STRIDE + j
                a = a + _mm3(xj + pe_ref[j:j + 1, :], w1_ref[j * DH:(j + 1) * DH, :])
                b = b + _mm3(xj + pe_ref[jb:jb + 1, :], w1_ref[jb * DH:(jb + 1) * DH, :])
            pre = a + pltpu.roll(b, NC - 1, axis=0)
            return _mm3(_silu(pre), w2_ref[...])

        kc_s[...] = compress(kc_ref, pek_ref, wk1_ref, wk2_ref)
        vc_s[...] = compress(vc_ref, pev_ref, wv1_ref, wv2_ref).astype(BF16)
        kw_s[0:WIN_LEN, :] = jnp.zeros((WIN_LEN, DH), BF16)
        kw_s[WIN_LEN:, :] = kw_ref[...]
        vt_s[...] = vs_ref[...].astype(F32).T.astype(BF16)
        vw_t = vw_ref[...].astype(F32).T.astype(BF16)
        for j in range(WIN_LEN // TQ):
            vwt_s[j] = jnp.zeros((DH, TQ), BF16)
        for j in range(S // TQ):
            vwt_s[WIN_LEN // TQ + j] = vw_t[:, j * TQ:(j + 1) * TQ]

    t0 = qi * TQ
    qb = q_ref[...].astype(F32) * (DH ** -0.5)
    q4 = jnp.concatenate([qb[:, r * DH:(r + 1) * DH] for r in range(R)], axis=0)
    q4b = q4.astype(BF16)

    def iota(shape, dim):
        return lax.broadcasted_iota(jnp.int32, shape, dim)

    sc = _mm_nt3(q4, kc_s[...])
    n_id = iota((R * TQ, NC), 1)
    t_c = t0 + (iota((R * TQ, NC), 0) & (TQ - 1))
    ok_c = (n_id * CMP_STRIDE + (CMP_LEN - 1) <= t_c) & (n_id < NC - 1)
    m_c = jnp.max(jnp.where(ok_c, sc, NEG_INF), axis=-1, keepdims=True)
    e_c = jnp.where(ok_c, jnp.exp(sc - m_c), 0.0)
    den_c = jnp.sum(e_c, axis=-1, keepdims=True)
    p_c = e_c / jnp.where(den_c > 0.0, den_c, 1.0)
    o_cmp = jnp.dot(p_c.astype(BF16), vc_s[...], preferred_element_type=F32)

    p_sum = p_c[0:TQ]
    for r in range(1, R):
        p_sum = p_sum + p_c[r * TQ:(r + 1) * TQ]
    js = iota((NB, NC), 0) * SLC_LEN
    cn = iota((NB, NC), 1) * CMP_STRIDE
    overlap_t = jnp.where((cn < js + SLC_LEN) & (cn + CMP_LEN > js), 1.0, 0.0).astype(BF16)
    imp = sum(_mm_nt(overlap_t, p) for p in _split2(p_sum))
    blk = iota((NB, TQ), 0)
    cur = (t0 + iota((NB, TQ), 1)) >> SLC_SHIFT
    forced = (blk == 0) | (blk == cur) | (blk == cur - 1)
    work = jnp.where(blk > cur, -FORCE, jnp.where(forced, FORCE, imp))
    blkf = blk.astype(F32)
    sel_bias = jnp.full((NB, TQ), NEG_INF, F32)
    for _ in range(SLC_TOPN):
        mx = jnp.max(work, axis=0, keepdims=True)
        idx = jnp.min(jnp.where(work == mx, blkf, float(NB)), axis=0, keepdims=True)
        hit = blkf == idx
        sel_bias = jnp.where(hit, 0.0, sel_bias)
        work = jnp.where(hit, -jnp.inf, work)
    sel_bias = sel_bias.astype(BF16)

    def selected(ne):
        expand = jnp.where((iota((NB, ne), 1) >> SLC_SHIFT) == iota((NB, ne), 0), 1.0, 0.0).astype(BF16)
        bias = _mm_tn(sel_bias, expand)
        bias = jnp.where(iota((TQ, ne), 1) <= t0 + iota((TQ, ne), 0), bias, NEG_INF)
        oslc_s[...] = attend(ks_ref[0:ne, :], vt_s[:, 0:ne], bias)

    def attend(k, v_t, bias):
        nk = k.shape[0]
        s = _mm_nt(q4b, k).reshape(R, TQ, nk) + bias[None]
        e = jnp.exp(s - jnp.max(s, axis=-1, keepdims=True))
        den = jnp.sum(e, axis=-1, keepdims=True)
        o_t = _mm_nt(v_t, e.reshape(R * TQ, nk))
        return jnp.concatenate([o_t[:, r * TQ:(r + 1) * TQ].T / den[r] for r in range(R)], axis=0)

    for vi in range(S // NSA_KEY_STEP):
        pl.when((t0 >= vi * NSA_KEY_STEP) & (t0 < (vi + 1) * NSA_KEY_STEP))(
            functools.partial(selected, (vi + 1) * NSA_KEY_STEP))

    span = WIN_LEN + TQ
    kw = kw_s[pl.ds(pl.multiple_of(t0, TQ), span), :]
    vw_t = jnp.concatenate([vwt_s[qi + j] for j in range(span // TQ)], axis=1)
    rel = iota((TQ, span), 1) - WIN_LEN - iota((TQ, span), 0)
    ok_w = (rel <= 0) & (rel > -WIN_LEN) & (t0 - WIN_LEN + iota((TQ, span), 1) >= 0)
    o_win = attend(kw, vw_t, jnp.where(ok_w, 0.0, NEG_INF))

    sg = sm_ref[...]
    sl = iota((1, DH), 1)
    for r in range(R):
        rows = slice(r * TQ, (r + 1) * TQ)
        base = SM_NGATE + (g * R + r) * 3
        gate = [jnp.sum(jnp.where(sl == base + i, sg, 0.0), axis=-1, keepdims=True) for i in range(3)]
        o_ref[:, r * DH:(r + 1) * DH] = (gate[0] * o_cmp[rows] + gate[1] * oslc_s[rows, :]
                                         + gate[2] * o_win[rows]).astype(o_ref.dtype)


def _nsa(u3, sm3, pe_k, pe_v, wk1, wk2, wv1, wv2):
    B, S, _ = u3.shape
    G, R, TQ = NSA_GROUPS, NSA_REP, NSA_TQ
    NC = S // CMP_STRIDE

    def kv(i):
        return pl.BlockSpec((None, S, DH), lambda b, g, t: (b, 0, OFF_NKV // DH + i * G + g))

    def full(shape):
        return pl.BlockSpec(shape, lambda b, g, t: (0,) * len(shape))

    return pl.pallas_call(
        _nsa_kernel,
        grid=(B, G, S // TQ),
        in_specs=[
            pl.BlockSpec((None, TQ, R * DH), lambda b, g, t: (b, t, OFF_NQ // (R * DH) + g)),
            kv(0), kv(1), kv(2), kv(3), kv(4), kv(5),
            pl.BlockSpec((None, TQ, DH), lambda b, g, t: (b, t, 0)),
            full((CMP_LEN, DH)), full((CMP_LEN, DH)),
            full((CMP_LEN * DH, DH)), full((DH, DH)), full((CMP_LEN * DH, DH)), full((DH, DH)),
        ],
        out_specs=pl.BlockSpec((None, TQ, R * DH), lambda b, g, t: (b, t, g)),
        out_shape=jax.ShapeDtypeStruct((B, S, HEADS * DH), BF16),
        scratch_shapes=[
            pltpu.VMEM((S, DH), F32),
            pltpu.VMEM((NC, DH), F32), pltpu.VMEM((NC, DH), BF16),
            pltpu.VMEM((S + WIN_LEN, DH), BF16), pltpu.VMEM((DH, S), BF16),
            pltpu.VMEM(((S + WIN_LEN) // TQ, DH, TQ), BF16),
            pltpu.VMEM((R * TQ, DH), F32),
        ],
        compiler_params=_cparams(("parallel", "parallel", "arbitrary")),
        name="nsa",
    )(u3, u3, u3, u3, u3, u3, u3, sm3, pe_k, pe_v, wk1, wk2, wv1, wv2)


def _merge_kernel(x_ref, ya_ref, yb_ref, oc_ref, hg_ref, hn_ref, ga_ref, gb_ref, gc_ref,
                  pa_ref, pb_ref, pc_ref, wo_ref, o_ref, yc_s, mg_s, *, nk):
    c = pl.program_id(1)
    tk = o_ref.shape[1]

    @pl.when(c == 0)
    def _():
        yc_s[...] = (_rms(oc_ref[...].astype(F32), hn_ref[...]) * _silu(hg_ref[...].astype(F32))).astype(BF16)

    @pl.when(c < nk)
    def _():
        m = (_sigmoid(ga_ref[...].astype(F32)) * jnp.dot(ya_ref[...], pa_ref[...], preferred_element_type=F32)
             + _sigmoid(gb_ref[...].astype(F32)) * jnp.dot(yb_ref[...], pb_ref[...], preferred_element_type=F32)
             + _sigmoid(gc_ref[...].astype(F32)) * jnp.dot(yc_s[...], pc_ref[...], preferred_element_type=F32))
        mg_s[c] = m.astype(BF16)

    @pl.when(c >= nk)
    def _():
        acc = x_ref[...]
        for j in range(nk):
            acc = acc + jnp.dot(mg_s[j], wo_ref[j * tk:(j + 1) * tk, :], preferred_element_type=F32)
        o_ref[...] = acc


def _merge(x, u, ya, yb, oc, hnorm, pa, pb, pc, wo, l, tm=1024, tk=512):
    T, D = x.shape
    V = ya.shape[1]
    nk = D // tk
    first = lambda c: jnp.minimum(c, nk - 1)
    second = lambda c: jnp.maximum(c - nk, 0)

    def gate(i):
        return pl.BlockSpec((tm, tk), lambda m, c: (m, (OFF_MG + i * D) // tk + first(c)))

    rows = lambda w: pl.BlockSpec((tm, w), lambda m, c: (m, 0))
    proj = pl.BlockSpec((None, V, tk), lambda m, c: (l, 0, first(c)))
    return pl.pallas_call(
        functools.partial(_merge_kernel, nk=nk),
        grid=(T // tm, 2 * nk),
        in_specs=[
            pl.BlockSpec((tm, tk), lambda m, c: (m, second(c))),
            rows(V), rows(V), rows(V),
            pl.BlockSpec((tm, V), lambda m, c: (m, OFF_HG // V)),
            pl.BlockSpec((1, V), lambda m, c: (0, 0)),
            gate(0), gate(1), gate(2),
            proj, proj, proj,
            pl.BlockSpec((None, D, tk), lambda m, c: (l, 0, second(c))),
        ],
        out_specs=pl.BlockSpec((tm, tk), lambda m, c: (m, second(c))),
        out_shape=jax.ShapeDtypeStruct((T, D), F32),
        scratch_shapes=[pltpu.VMEM((tm, V), BF16), pltpu.VMEM((nk, tm, tk), BF16)],
        compiler_params=_cparams(("parallel", "arbitrary")),
        name="merge",
    )(x, ya, yb, oc, u, hnorm, u, u, u, pa, pb, pc, wo)


SRC_QKV = (0, 3072)
SRC_BA = (3072, 3088)
SRC_MID = (3088, 6672)
SRC_NGATE = (6672, 6696)
SRC_TAIL = (6696, 16936)


RELAYOUT_ROWS = 512


def _relayout_kernel(wt_ref, o_ref):
    cb = wt_ref.shape[1]

    def put(dst, src):
        for s in range(src[0], src[1], RELAYOUT_ROWS):
            d = dst + s - src[0]
            o_ref[:, d:d + RELAYOUT_ROWS] = wt_ref[s:s + RELAYOUT_ROWS, :].T.astype(BF16)

    put(OFF_GQ, SRC_QKV)
    put(OFF_GG, SRC_MID)
    put(OFF_HQ, SRC_TAIL)
    small = jnp.concatenate([wt_ref[SRC_BA[0]:SRC_BA[1], :], wt_ref[SRC_NGATE[0]:SRC_NGATE[1], :],
                             jnp.zeros((DH - (SM_NGATE + 3 * HEADS), cb), F32)], axis=0)
    o_ref[:, OFF_SM:OFF_SM + DH] = small.T.astype(BF16)
    o_ref[:, OFF_SM + DH:OFF_HQ] = jnp.zeros((cb, OFF_HQ - OFF_SM - DH), BF16)


def _relayout_w_in(w_t, cb=128):
    L, n_in, D = w_t.shape
    return pl.pallas_call(
        _relayout_kernel,
        grid=(L, D // cb),
        in_specs=[pl.BlockSpec((None, n_in, cb), lambda l, r: (l, 0, r))],
        out_specs=pl.BlockSpec((None, cb, N_U), lambda l, r: (l, r, 0)),
        out_shape=jax.ShapeDtypeStruct((L, D, N_U), BF16),
        compiler_params=_cparams(("parallel", "parallel")),
        name="relayout_w_in",
    )(w_t)


def kernel(x, ffn1_norm, ffn1_w_gate, ffn1_w_up, ffn1_w_down, mix_norm, w_in, gdn_conv, gdn_a_log, gdn_dt_bias, gdn_out_norm, nsa_cmp_pe_k, nsa_cmp_pe_v, nsa_cmp_k_w1, nsa_cmp_k_w2, nsa_cmp_v_w1, nsa_cmp_v_w2, hgrn_lb_logits, hgrn_out_norm, w_proj_a, w_proj_b, w_proj_c, w_out, ffn2_norm, ffn2_w_gate, ffn2_w_up, ffn2_w_down, final_norm):
    B, S, D = x.shape
    L = w_in.shape[0]
    T = B * S
    row = lambda v: v.reshape(1, -1).astype(F32)
    bf = lambda w: w.astype(BF16)
    small_row = lambda v: jnp.zeros((1, DH), F32).at[0, SM_A:SM_A + HEADS].set(v.astype(F32))

    lb_p = jax.nn.softmax(hgrn_lb_logits.astype(F32), axis=0)
    lower_bounds = jnp.cumsum(lb_p, axis=0) - lb_p[0]
    fin = row(final_norm)
    f1 = (ffn1_w_gate, ffn1_w_up, ffn1_w_down)
    f2 = (ffn2_w_gate, ffn2_w_up, ffn2_w_down)
    w_in_r = _relayout_w_in(jnp.swapaxes(w_in.astype(F32), 1, 2))
    projs = (bf(w_proj_a), bf(w_proj_b), bf(w_proj_c), bf(w_out))

    xf = x.reshape(T, D)
    for l in range(L):
        xf = _ffn(xf, row(ffn1_norm[l]), *f1, fin, l, False)

        u, sm = _inproj(xf, row(mix_norm[l]), w_in_r, small_row(gdn_a_log[l]), small_row(gdn_dt_bias[l]), l)
        u3 = u.reshape(B, S, N_U)
        sm3 = sm.reshape(B, S, DH)
        ya = _gdn(u3, sm3, gdn_conv[l].astype(F32), row(gdn_out_norm[l]))
        yb = _nsa(u3, sm3, nsa_cmp_pe_k[l].astype(F32), nsa_cmp_pe_v[l].astype(F32),
                  nsa_cmp_k_w1[l], nsa_cmp_k_w2[l], nsa_cmp_v_w1[l], nsa_cmp_v_w2[l])
        oc = _hgrn(u3, lower_bounds[l].reshape(HEADS, 1, DH))
        xf = _merge(xf, u, ya.reshape(T, -1), yb.reshape(T, -1), oc.reshape(T, -1), row(hgrn_out_norm[l]),
                    *projs, l)

        xf = _ffn(xf, row(ffn2_norm[l]), *f2, fin, l, l == L - 1)
    return xf.reshape(B, S, D)
```

```python
import functools

import jax
import jax.numpy as jnp
from jax import lax
from jax.experimental import pallas as pl
from jax.experimental.pallas import tpu as pltpu

F32 = jnp.float32
BF16 = jnp.bfloat16

D_MODEL = 2048
D_FF = 5632
NORM_EPS = 1e-6
NEG_INF = -1e30
FORCE = 1e6
LOG_FLOOR = 1e-30

HEADS = 8
DH = 128
CHUNK = 64
GDN_CONV = 4
GDN_HPS = 4
NSA_GROUPS = 2
NSA_REP = HEADS // NSA_GROUPS
CMP_LEN = 32
CMP_STRIDE = 16
SLC_LEN = 64
SLC_SHIFT = 6
SLC_TOPN = 8
WIN_LEN = 512
NSA_TQ = 128
NSA_KEY_STEP = 256

OFF_GQ, OFF_GK, OFF_GV, OFF_GG = 0, 1024, 2048, 3072
OFF_NQ, OFF_NKV, OFF_SM = 4096, 5120, 6656
OFF_HQ, OFF_HF, OFF_HI, OFF_HG = 7168, 8192, 9216, 10240
OFF_MG = 11264
N_U = 17408
SM_BETA, SM_A, SM_NGATE = 0, 8, 16

V7X_VMEM_BYTES = 64 * 1024 * 1024
VMEM_LIMIT = V7X_VMEM_BYTES * 7 // 8

FFN_TM, FFN_TF = 1024, 256
INPROJ_TM, INPROJ_TN = 1024, 1024
MERGE_TM, MERGE_TK = 1024, 512
RELAYOUT_CB = 128


def _cparams(sem):
    return pltpu.CompilerParams(dimension_semantics=sem, vmem_limit_bytes=VMEM_LIMIT)


def _sigmoid(x):
    return 1.0 / (1.0 + jnp.exp(-x))


def _silu(x):
    return x * _sigmoid(x)


def _rms(x, g):
    return x * lax.rsqrt(jnp.mean(x * x, axis=-1, keepdims=True) + NORM_EPS) * g


def _mm_nt(a, b):
    return lax.dot_general(a.astype(BF16), b.astype(BF16), (((1,), (1,)), ((), ())),
                           preferred_element_type=F32)


def _mm_tn(a, b):
    return lax.dot_general(a.astype(BF16), b.astype(BF16), (((0,), (0,)), ((), ())),
                           preferred_element_type=F32)


def _bmm(a, b):
    return jnp.einsum('cmk,ckn->cmn', a.astype(BF16), b.astype(BF16), preferred_element_type=F32)


def _bmm_nt(a, b):
    return jnp.einsum('cmk,cnk->cmn', a.astype(BF16), b.astype(BF16), preferred_element_type=F32)


def _bmm_tn(a, b):
    return jnp.einsum('ckm,ckn->cmn', a.astype(BF16), b.astype(BF16), preferred_element_type=F32)


def _split2(x):
    hi = x.astype(BF16)
    return hi, (x - hi.astype(F32)).astype(BF16)


def _mm_nt3(a, b):
    (ah, al), (bh, bl) = _split2(a), _split2(b)
    return _mm_nt(ah, bh) + (_mm_nt(ah, bl) + _mm_nt(al, bh))


def _mm3(a, b):
    (ah, al), (bh, bl) = _split2(a), _split2(b)
    d = lambda p, r: jnp.dot(p, r, preferred_element_type=F32)
    return d(ah, bh) + (d(ah, bl) + d(al, bh))


def _ffn_kernel(x_ref, g_ref, wg_ref, wu_ref, wd_ref, fg_ref, o_ref, h_ref, *, final_norm):
    f = pl.program_id(1)

    @pl.when(f == 0)
    def _():
        x = x_ref[...]
        h_ref[...] = _rms(x, g_ref[...]).astype(BF16)
        o_ref[...] = x

    h = h_ref[...]
    a = jnp.dot(h, wg_ref[...].astype(BF16), preferred_element_type=F32)
    b = jnp.dot(h, wu_ref[...].astype(BF16), preferred_element_type=F32)
    act = (_silu(a) * b * 0.5).astype(BF16)
    o_ref[...] += jnp.dot(act, wd_ref[...].astype(BF16), preferred_element_type=F32)

    if final_norm:
        @pl.when(f == pl.num_programs(1) - 1)
        def _():
            o_ref[...] = _rms(o_ref[...], fg_ref[...])


def _ffn(x, g, wg, wu, wd, fg, l, final_norm, tm=FFN_TM, tf=FFN_TF):
    T, D = x.shape
    F = wg.shape[2]
    return pl.pallas_call(
        functools.partial(_ffn_kernel, final_norm=final_norm),
        grid=(T // tm, F // tf),
        in_specs=[
            pl.BlockSpec((tm, D), lambda i, f: (i, 0)),
            pl.BlockSpec((1, D), lambda i, f: (0, 0)),
            pl.BlockSpec((None, D, tf), lambda i, f: (l, 0, f)),
            pl.BlockSpec((None, D, tf), lambda i, f: (l, 0, f)),
            pl.BlockSpec((None, tf, D), lambda i, f: (l, f, 0)),
            pl.BlockSpec((1, D), lambda i, f: (0, 0)),
        ],
        out_specs=pl.BlockSpec((tm, D), lambda i, f: (i, 0)),
        out_shape=jax.ShapeDtypeStruct((T, D), F32),
        scratch_shapes=[pltpu.VMEM((tm, D), BF16)],
        compiler_params=_cparams(("parallel", "arbitrary")),
        name="ffn",
    )(x, g, wg, wu, wd, fg)


def _inproj_kernel(x_ref, g_ref, w_ref, alog_ref, dtb_ref, o_ref, sm_ref, h_ref, *, tn):
    n = pl.program_id(1)

    @pl.when(n == 0)
    def _():
        h_ref[...] = _rms(x_ref[...], g_ref[...]).astype(BF16)

    acc = jnp.dot(h_ref[...], w_ref[...], preferred_element_type=F32)
    o_ref[...] = acc.astype(o_ref.dtype)

    @pl.when(n == OFF_SM // tn)
    def _():
        sm = acc[:, OFF_SM % tn:OFF_SM % tn + DH]
        lane = lax.broadcasted_iota(jnp.int32, (1, DH), 1)
        z = sm + dtb_ref[...]
        softplus = jnp.maximum(z, 0.0) + jnp.log1p(jnp.exp(-jnp.abs(z)))
        sm_ref[...] = jnp.where((lane >= SM_A) & (lane < SM_A + HEADS),
                                -jnp.exp(alog_ref[...]) * softplus, _sigmoid(sm))


def _inproj(x, g, w, a_log_row, dt_bias_row, l, tm=INPROJ_TM, tn=INPROJ_TN):
    T, D = x.shape
    N = w.shape[2]
    return pl.pallas_call(
        functools.partial(_inproj_kernel, tn=tn),
        grid=(T // tm, N // tn),
        in_specs=[
            pl.BlockSpec((tm, D), lambda i, n: (i, 0)),
            pl.BlockSpec((1, D), lambda i, n: (0, 0)),
            pl.BlockSpec((None, D, tn), lambda i, n: (l, 0, n)),
            pl.BlockSpec((1, DH), lambda i, n: (0, 0)),
            pl.BlockSpec((1, DH), lambda i, n: (0, 0)),
        ],
        out_specs=[pl.BlockSpec((tm, tn), lambda i, n: (i, n)),
                   pl.BlockSpec((tm, DH), lambda i, n: (i, 0))],
        out_shape=[jax.ShapeDtypeStruct((T, N), BF16), jax.ShapeDtypeStruct((T, DH), F32)],
        scratch_shapes=[pltpu.VMEM((tm, D), BF16)],
        compiler_params=_cparams(("parallel", "arbitrary")),
        name="in_proj",
    )(x, g, w, a_log_row, dt_bias_row)


def _gdn_kernel(q_ref, k_ref, v_ref, gate_ref, sm_ref, cq_ref, ck_ref, cv_ref, ng_ref,
                o_ref, kwt_s, bt_s, sp_s, eg_s, o0_s, qe_s):
    hp = pl.program_id(1)
    S = q_ref.shape[0]
    n = S // CHUNK
    C = CHUNK
    row8 = lax.broadcasted_iota(jnp.int32, (8, DH), 0)
    ti = lax.broadcasted_iota(jnp.int32, (1, C, C), 1)
    si = lax.broadcasted_iota(jnp.int32, (1, C, C), 2)
    lane = lax.broadcasted_iota(jnp.int32, (1, 1, DH), 2)
    causal = ti >= si
    strict = ti > si
    tril = jnp.broadcast_to(jnp.where(causal, 1.0, 0.0).astype(BF16), (n, C, C))
    eye = jnp.where(ti == si, 1.0, 0.0)

    def conv_silu(x, w):
        taps = [w[GDN_CONV - 1 - j:GDN_CONV - j] for j in range(GDN_CONV)]
        y = x * taps[0]
        for j in range(1, GDN_CONV):
            y = y + pltpu.roll(x, j, axis=0) * taps[j]
        x8 = x[0:8]
        y8 = x8 * taps[0]
        for j in range(1, GDN_CONV):
            y8 = y8 + jnp.where(row8 >= j, pltpu.roll(x8, j, axis=0), 0.0) * taps[j]
        return _silu(jnp.concatenate([y8, y[8:]], axis=0))

    def l2n(x):
        return x * lax.rsqrt(jnp.sum(x * x, axis=-1, keepdims=True) + 1e-6)

    def prepare(i):
        cols = slice(i * DH, (i + 1) * DH)
        h = hp * GDN_HPS + i
        q = l2n(conv_silu(q_ref[:, cols].astype(F32), cq_ref[:, cols])) * (DH ** -0.5)
        k = l2n(conv_silu(k_ref[:, cols].astype(F32), ck_ref[:, cols]))
        v = conv_silu(v_ref[:, cols].astype(F32), cv_ref[:, cols])
        q3, k3, v3 = (t.reshape(n, C, DH) for t in (q, k, v))

        sm3 = sm_ref[...].reshape(n, C, DH)
        beta3 = jnp.sum(jnp.where(lane == SM_BETA + h, sm3, 0.0), axis=-1, keepdims=True)
        g3 = jnp.sum(jnp.where(lane == SM_A + h, sm3, 0.0), axis=-1, keepdims=True)

        gc3 = sum(_bmm(tril, p) for p in _split2(jnp.broadcast_to(g3, (n, C, DH))))
        gc_row = jnp.swapaxes(gc3, 1, 2)[:, :C, :]
        decay = jnp.where(causal, jnp.exp(jnp.where(causal, gc3[:, :, :C] - gc_row, 0.0)), 0.0)

        kb3 = k3 * beta3
        m = jnp.where(strict, _bmm_nt(kb3, k3) * decay, 0.0)

        def lower_left(blk):
            half = blk // 2
            pair = ((ti & (blk - 1)) >= half) & ((si & (blk - 1)) < half)
            if blk < C:
                pair = pair & ((ti & -blk) == (si & -blk))
            return jnp.where(pair, m, 0.0)

        x = eye - lower_left(2)
        for blk in (4, 8, 16, 32, 64):
            x = x - _bmm(_bmm(x, lower_left(blk)), x)
        egc = jnp.exp(gc3)
        uw = _bmm(x, jnp.concatenate([v3 * beta3, kb3 * egc], axis=-1))
        attn = jnp.where(causal, _bmm_nt(q3, k3) * decay, 0.0)
        g_last = gc3[:, C - 1:C, :]
        kg = k3 * jnp.exp(g_last - gc3)
        au = _bmm(attn, uw)
        o0_s[i] = au[:, :, :DH]
        qe_s[i] = (q3 * egc - au[:, :, DH:]).astype(BF16)
        kwb = _bmm_tn(uw, kg)
        bt_s[i] = kwb[:, :DH, :]
        kwt_s[i] = kwb[:, DH:, :].astype(BF16)
        eg_s[i] = jnp.exp(g_last)

    for i in range(GDN_HPS):
        prepare(i)

    def body(c, sts):
        nxt = []
        for i, st in enumerate(sts):
            stb = st.astype(BF16)
            sp_s[i, c] = stb
            nxt.append(st * eg_s[i, c] - jnp.dot(stb, kwt_s[i, c], preferred_element_type=F32) + bt_s[i, c])
        return tuple(nxt)

    lax.fori_loop(0, n, body, tuple(jnp.zeros((DH, DH), F32) for _ in range(GDN_HPS)))

    for i in range(GDN_HPS):
        cols = slice(i * DH, (i + 1) * DH)
        o = (o0_s[i] + _bmm_nt(qe_s[i], sp_s[i])).reshape(S, DH)
        o_ref[:, cols] = (_rms(o, ng_ref[...]) * _silu(gate_ref[:, cols].astype(F32))).astype(o_ref.dtype)


def _gdn(u3, sm3, conv_w, norm_g):
    B, S, _ = u3.shape
    n = S // CHUNK
    W = GDN_HPS * DH

    def col(off):
        return pl.BlockSpec((None, S, W), lambda b, h: (b, 0, off // W + h))

    def cw(off):
        return pl.BlockSpec((GDN_CONV, W), lambda b, h: (0, off // W + h))

    return pl.pallas_call(
        _gdn_kernel,
        grid=(B, HEADS // GDN_HPS),
        in_specs=[
            col(OFF_GQ), col(OFF_GK), col(OFF_GV), col(OFF_GG),
            pl.BlockSpec((None, S, DH), lambda b, h: (b, 0, 0)),
            cw(OFF_GQ), cw(OFF_GK), cw(OFF_GV),
            pl.BlockSpec((1, DH), lambda b, h: (0, 0)),
        ],
        out_specs=pl.BlockSpec((None, S, W), lambda b, h: (b, 0, h)),
        out_shape=jax.ShapeDtypeStruct((B, S, HEADS * DH), BF16),
        scratch_shapes=[
            pltpu.VMEM((GDN_HPS, n, DH, DH), BF16), pltpu.VMEM((GDN_HPS, n, DH, DH), F32),
            pltpu.VMEM((GDN_HPS, n, DH, DH), BF16), pltpu.VMEM((GDN_HPS, n, 1, DH), F32),
            pltpu.VMEM((GDN_HPS, n, CHUNK, DH), F32), pltpu.VMEM((GDN_HPS, n, CHUNK, DH), BF16),
        ],
        compiler_params=_cparams(("parallel", "arbitrary")),
        name="gdn",
    )(u3, u3, u3, u3, sm3, conv_w, conv_w, conv_w, norm_g)


def _hgrn_kernel(q_ref, f_ref, i_ref, lb_ref, o_ref, kv_s, sp_s, eb_s):
    S = q_ref.shape[0]
    C = CHUNK
    n = S // C
    lb = lb_ref[...]
    q3 = _silu(q_ref[...].astype(F32)).reshape(n, C, DH)
    f = lb + (1.0 - lb) * _sigmoid(f_ref[...].astype(F32))
    logf = jnp.log(jnp.maximum(f, LOG_FLOOR))
    k3 = (1.0 - f).reshape(n, C, DH)
    v3 = i_ref[...].reshape(n, C, DH)

    ti = lax.broadcasted_iota(jnp.int32, (1, C, C), 1)
    si = lax.broadcasted_iota(jnp.int32, (1, C, C), 2)
    row = lax.broadcasted_iota(jnp.int32, (1, C, DH), 1)
    tril = jnp.broadcast_to(jnp.where(ti >= si, 1.0, 0.0).astype(BF16), (n, C, C))
    b3 = sum(_bmm(tril, p) for p in _split2(logf.reshape(n, C, DH)))
    b = b3.reshape(S, DH)

    def shifted(t, shift):
        return pltpu.roll(t.reshape(S, DH), shift, axis=0).reshape(n, C, DH)

    def mid_rows(blk):
        half = blk // 2
        if blk >= 8:
            t4 = b.reshape(S // blk, blk, DH)
            return jnp.broadcast_to(t4[:, half:half + 1, :], t4.shape).reshape(n, C, DH)
        if blk == 4:
            d = jnp.where((row & 1) == 1, shifted(b3, 1), b3)
            return jnp.where((row & 3) < 2, shifted(d, S - 2), d)
        return jnp.where((row & 1) == 0, shifted(b3, S - 1), b3)

    a = jnp.where(ti == si, jnp.sum(q3 * k3, axis=-1, keepdims=True), 0.0)
    for blk in (64, 32, 16, 8, 4, 2):
        half = blk // 2
        upper = (row & (blk - 1)) >= half
        x = jnp.where(upper, q3, k3) * jnp.exp(-jnp.abs(b3 - mid_rows(blk)))
        pair = ((ti & (blk - 1)) >= half) & ((si & (blk - 1)) < half)
        if blk < C:
            pair = pair & ((ti & -blk) == (si & -blk))
        a = jnp.where(pair, _bmm_nt(x, x), a)

    o_intra = _bmm(a, v3)
    b_last = b3[:, C - 1:C, :]
    kv_s[...] = _bmm_tn(v3, k3 * jnp.exp(b_last - b3))
    eb_s[...] = jnp.exp(b_last)

    def body(c, st):
        sp_s[c] = st.astype(BF16)
        return st * eb_s[c] + kv_s[c]

    lax.fori_loop(0, n, body, jnp.zeros((DH, DH), F32))
    o_ref[...] = (o_intra + _bmm_nt(q3 * jnp.exp(b3), sp_s[...])).reshape(S, DH).astype(o_ref.dtype)


def _hgrn(u3, lb_b):
    B, S, _ = u3.shape
    n = S // CHUNK

    def col(off):
        return pl.BlockSpec((None, S, DH), lambda b, h: (b, 0, off // DH + h))

    return pl.pallas_call(
        _hgrn_kernel,
        grid=(B, HEADS),
        in_specs=[col(OFF_HQ), col(OFF_HF), col(OFF_HI),
                  pl.BlockSpec((None, 1, DH), lambda b, h: (h, 0, 0))],
        out_specs=pl.BlockSpec((None, S, DH), lambda b, h: (b, 0, h)),
        out_shape=jax.ShapeDtypeStruct((B, S, HEADS * DH), BF16),
        scratch_shapes=[
            pltpu.VMEM((n, DH, DH), F32), pltpu.VMEM((n, DH, DH), BF16), pltpu.VMEM((n, 1, DH), F32),
        ],
        compiler_params=_cparams(("parallel", "arbitrary")),
        name="hgrn2",
    )(u3, u3, u3, lb_b)


def _nsa_kernel(q_ref, kc_ref, vc_ref, ks_ref, vs_ref, kw_ref, vw_ref, sm_ref,
                pek_ref, pev_ref, wk1_ref, wk2_ref, wv1_ref, wv2_ref,
                o_ref, x_s, kc_s, vc_s, kw_s, vt_s, vwt_s, oslc_s):
    g = pl.program_id(1)
    qi = pl.program_id(2)
    S = ks_ref.shape[0]
    TQ = NSA_TQ
    R = NSA_REP
    NC = S // CMP_STRIDE
    NB = S // SLC_LEN

    @pl.when(qi == 0)
    def _():
        def compress(x_ref, pe_ref, w1_ref, w2_ref):
            x_s[...] = x_ref[...].astype(F32)
            a = jnp.zeros((NC, DH), F32)
            b = jnp.zeros((NC, DH), F32)
            for j in range(CMP_STRIDE):
                xj = x_s[pl.ds(j, NC, stride=CMP_STRIDE), :]
                jb = CMP_STRIDE + j
                a = a + _mm3(xj + pe_ref[j:j + 1, :], w1_ref[j * DH:(j + 1) * DH, :])
                b = b + _mm3(xj + pe_ref[jb:jb + 1, :], w1_ref[jb * DH:(jb + 1) * DH, :])
            pre = a + pltpu.roll(b, NC - 1, axis=0)
            return _mm3(_silu(pre), w2_ref[...])

        kc_s[...] = compress(kc_ref, pek_ref, wk1_ref, wk2_ref)
        vc_s[...] = compress(vc_ref, pev_ref, wv1_ref, wv2_ref).astype(BF16)
        kw_s[0:WIN_LEN, :] = jnp.zeros((WIN_LEN, DH), BF16)
        kw_s[WIN_LEN:, :] = kw_ref[...]
        vt_s[...] = vs_ref[...].astype(F32).T.astype(BF16)
        vw_t = vw_ref[...].astype(F32).T.astype(BF16)
        for j in range(WIN_LEN // TQ):
            vwt_s[j] = jnp.zeros((DH, TQ), BF16)
        for j in range(S // TQ):
            vwt_s[WIN_LEN // TQ + j] = vw_t[:, j * TQ:(j + 1) * TQ]

    t0 = qi * TQ
    qb = q_ref[...].astype(F32) * (DH ** -0.5)
    q4 = jnp.concatenate([qb[:, r * DH:(r + 1) * DH] for r in range(R)], axis=0)
    q4b = q4.astype(BF16)

    def iota(shape, dim):
        return lax.broadcasted_iota(jnp.int32, shape, dim)

    sc = _mm_nt3(q4, kc_s[...])
    n_id = iota((R * TQ, NC), 1)
    t_c = t0 + (iota((R * TQ, NC), 0) & (TQ - 1))
    ok_c = (n_id * CMP_STRIDE + (CMP_LEN - 1) <= t_c) & (n_id < NC - 1)
    m_c = jnp.max(jnp.where(ok_c, sc, NEG_INF), axis=-1, keepdims=True)
    e_c = jnp.where(ok_c, jnp.exp(sc - m_c), 0.0)
    den_c = jnp.sum(e_c, axis=-1, keepdims=True)
    p_c = e_c / jnp.where(den_c > 0.0, den_c, 1.0)
    o_cmp = jnp.dot(p_c.astype(BF16), vc_s[...], preferred_element_type=F32)

    p_sum = p_c[0:TQ]
    for r in range(1, R):
        p_sum = p_sum + p_c[r * TQ:(r + 1) * TQ]
    js = iota((NB, NC), 0) * SLC_LEN
    cn = iota((NB, NC), 1) * CMP_STRIDE
    overlap_t = jnp.where((cn < js + SLC_LEN) & (cn + CMP_LEN > js), 1.0, 0.0).astype(BF16)
    imp = sum(_mm_nt(overlap_t, p) for p in _split2(p_sum))
    blk = iota((NB, TQ), 0)
    cur = (t0 + iota((NB, TQ), 1)) >> SLC_SHIFT
    forced = (blk == 0) | (blk == cur) | (blk == cur - 1)
    work = jnp.where(blk > cur, -FORCE, jnp.where(forced, FORCE, imp))
    blkf = blk.astype(F32)
    sel_bias = jnp.full((NB, TQ), NEG_INF, F32)
    for _ in range(SLC_TOPN):
        mx = jnp.max(work, axis=0, keepdims=True)
        idx = jnp.min(jnp.where(work == mx, blkf, float(NB)), axis=0, keepdims=True)
        hit = blkf == idx
        sel_bias = jnp.where(hit, 0.0, sel_bias)
        work = jnp.where(hit, -jnp.inf, work)
    sel_bias = sel_bias.astype(BF16)

    def selected(ne):
        expand = jnp.where((iota((NB, ne), 1) >> SLC_SHIFT) == iota((NB, ne), 0), 1.0, 0.0).astype(BF16)
        bias = _mm_tn(sel_bias, expand)
        bias = jnp.where(iota((TQ, ne), 1) <= t0 + iota((TQ, ne), 0), bias, NEG_INF)
        oslc_s[...] = attend(ks_ref[0:ne, :], vt_s[:, 0:ne], bias)

    def attend(k, v_t, bias):
        nk = k.shape[0]
        s = _mm_nt(q4b, k).reshape(R, TQ, nk) + bias[None]
        e = jnp.exp(s - jnp.max(s, axis=-1, keepdims=True))
        den = jnp.sum(e, axis=-1, keepdims=True)
        o_t = _mm_nt(v_t, e.reshape(R * TQ, nk))
        return jnp.concatenate([o_t[:, r * TQ:(r + 1) * TQ].T / den[r] for r in range(R)], axis=0)

    for vi in range(S // NSA_KEY_STEP):
        pl.when((t0 >= vi * NSA_KEY_STEP) & (t0 < (vi + 1) * NSA_KEY_STEP))(
            functools.partial(selected, (vi + 1) * NSA_KEY_STEP))

    span = WIN_LEN + TQ
    kw = kw_s[pl.ds(pl.multiple_of(t0, TQ), span), :]
    vw_t = jnp.concatenate([vwt_s[qi + j] for j in range(span // TQ)], axis=1)
    rel = iota((TQ, span), 1) - WIN_LEN - iota((TQ, span), 0)
    ok_w = (rel <= 0) & (rel > -WIN_LEN) & (t0 - WIN_LEN + iota((TQ, span), 1) >= 0)
    o_win = attend(kw, vw_t, jnp.where(ok_w, 0.0, NEG_INF))

    sg = sm_ref[...]
    sl = iota((1, DH), 1)
    for r in range(R):
        rows = slice(r * TQ, (r + 1) * TQ)
        base = SM_NGATE + (g * R + r) * 3
        gate = [jnp.sum(jnp.where(sl == base + i, sg, 0.0), axis=-1, keepdims=True) for i in range(3)]
        o_ref[:, r * DH:(r + 1) * DH] = (gate[0] * o_cmp[rows] + gate[1] * oslc_s[rows, :]
                                         + gate[2] * o_win[rows]).astype(o_ref.dtype)


def _nsa(u3, sm3, pe_k, pe_v, wk1, wk2, wv1, wv2):
    B, S, _ = u3.shape
    G, R, TQ = NSA_GROUPS, NSA_REP, NSA_TQ
    NC = S // CMP_STRIDE

    def kv(i):
        return pl.BlockSpec((None, S, DH), lambda b, g, t: (b, 0, OFF_NKV // DH + i * G + g))

    def full(shape):
        return pl.BlockSpec(shape, lambda b, g, t: (0,) * len(shape))

    return pl.pallas_call(
        _nsa_kernel,
        grid=(B, G, S // TQ),
        in_specs=[
            pl.BlockSpec((None, TQ, R * DH), lambda b, g, t: (b, t, OFF_NQ // (R * DH) + g)),
            kv(0), kv(1), kv(2), kv(3), kv(4), kv(5),
            pl.BlockSpec((None, TQ, DH), lambda b, g, t: (b, t, 0)),
            full((CMP_LEN, DH)), full((CMP_LEN, DH)),
            full((CMP_LEN * DH, DH)), full((DH, DH)), full((CMP_LEN * DH, DH)), full((DH, DH)),
        ],
        out_specs=pl.BlockSpec((None, TQ, R * DH), lambda b, g, t: (b, t, g)),
        out_shape=jax.ShapeDtypeStruct((B, S, HEADS * DH), BF16),
        scratch_shapes=[
            pltpu.VMEM((S, DH), F32),
            pltpu.VMEM((NC, DH), F32), pltpu.VMEM((NC, DH), BF16),
            pltpu.VMEM((S + WIN_LEN, DH), BF16), pltpu.VMEM((DH, S), BF16),
            pltpu.VMEM(((S + WIN_LEN) // TQ, DH, TQ), BF16),
            pltpu.VMEM((R * TQ, DH), F32),
        ],
        compiler_params=_cparams(("parallel", "parallel", "arbitrary")),
        name="nsa",
    )(u3, u3, u3, u3, u3, u3, u3, sm3, pe_k, pe_v, wk1, wk2, wv1, wv2)


def _merge_kernel(x_ref, ya_ref, yb_ref, oc_ref, hg_ref, hn_ref, ga_ref, gb_ref, gc_ref,
                  pa_ref, pb_ref, pc_ref, wo_ref, o_ref, yc_s, mg_s, *, nk):
    c = pl.program_id(1)
    tk = o_ref.shape[1]

    @pl.when(c == 0)
    def _():
        yc_s[...] = (_rms(oc_ref[...].astype(F32), hn_ref[...]) * _silu(hg_ref[...].astype(F32))).astype(BF16)

    @pl.when(c < nk)
    def _():
        m = (_sigmoid(ga_ref[...].astype(F32)) * jnp.dot(ya_ref[...], pa_ref[...], preferred_element_type=F32)
             + _sigmoid(gb_ref[...].astype(F32)) * jnp.dot(yb_ref[...], pb_ref[...], preferred_element_type=F32)
             + _sigmoid(gc_ref[...].astype(F32)) * jnp.dot(yc_s[...], pc_ref[...], preferred_element_type=F32))
        mg_s[c] = m.astype(BF16)

    @pl.when(c >= nk)
    def _():
        acc = x_ref[...]
        for j in range(nk):
            acc = acc + jnp.dot(mg_s[j], wo_ref[j * tk:(j + 1) * tk, :], preferred_element_type=F32)
        o_ref[...] = acc


def _merge(x, u, ya, yb, oc, hnorm, pa, pb, pc, wo, l, tm=MERGE_TM, tk=MERGE_TK):
    T, D = x.shape
    V = ya.shape[1]
    nk = D // tk
    first = lambda c: jnp.minimum(c, nk - 1)
    second = lambda c: jnp.maximum(c - nk, 0)

    def gate(i):
        return pl.BlockSpec((tm, tk), lambda m, c: (m, (OFF_MG + i * D) // tk + first(c)))

    rows = lambda w: pl.BlockSpec((tm, w), lambda m, c: (m, 0))
    proj = pl.BlockSpec((None, V, tk), lambda m, c: (l, 0, first(c)))
    return pl.pallas_call(
        functools.partial(_merge_kernel, nk=nk),
        grid=(T // tm, 2 * nk),
        in_specs=[
            pl.BlockSpec((tm, tk), lambda m, c: (m, second(c))),
            rows(V), rows(V), rows(V),
            pl.BlockSpec((tm, V), lambda m, c: (m, OFF_HG // V)),
            pl.BlockSpec((1, V), lambda m, c: (0, 0)),
            gate(0), gate(1), gate(2),
            proj, proj, proj,
            pl.BlockSpec((None, D, tk), lambda m, c: (l, 0, second(c))),
        ],
        out_specs=pl.BlockSpec((tm, tk), lambda m, c: (m, second(c))),
        out_shape=jax.ShapeDtypeStruct((T, D), F32),
        scratch_shapes=[pltpu.VMEM((tm, V), BF16), pltpu.VMEM((nk, tm, tk), BF16)],
        compiler_params=_cparams(("parallel", "arbitrary")),
        name="merge",
    )(x, ya, yb, oc, u, hnorm, u, u, u, pa, pb, pc, wo)


SRC_QKV = (0, 3072)
SRC_BA = (3072, 3088)
SRC_MID = (3088, 6672)
SRC_NGATE = (6672, 6696)
SRC_TAIL = (6696, 16936)


RELAYOUT_ROWS = 512


def _relayout_kernel(wt_ref, o_ref):
    cb = wt_ref.shape[1]

    def put(dst, src):
        for s in range(src[0], src[1], RELAYOUT_ROWS):
            d = dst + s - src[0]
            o_ref[:, d:d + RELAYOUT_ROWS] = wt_ref[s:s + RELAYOUT_ROWS, :].T.astype(BF16)

    put(OFF_GQ, SRC_QKV)
    put(OFF_GG, SRC_MID)
    put(OFF_HQ, SRC_TAIL)
    small = jnp.concatenate([wt_ref[SRC_BA[0]:SRC_BA[1], :], wt_ref[SRC_NGATE[0]:SRC_NGATE[1], :],
                             jnp.zeros((DH - (SM_NGATE + 3 * HEADS), cb), F32)], axis=0)
    o_ref[:, OFF_SM:OFF_SM + DH] = small.T.astype(BF16)
    o_ref[:, OFF_SM + DH:OFF_HQ] = jnp.zeros((cb, OFF_HQ - OFF_SM - DH), BF16)


def _relayout_w_in(w_t, cb=RELAYOUT_CB):
    L, n_in, D = w_t.shape
    return pl.pallas_call(
        _relayout_kernel,
        grid=(L, D // cb),
        in_specs=[pl.BlockSpec((None, n_in, cb), lambda l, r: (l, 0, r))],
        out_specs=pl.BlockSpec((None, cb, N_U), lambda l, r: (l, r, 0)),
        out_shape=jax.ShapeDtypeStruct((L, D, N_U), BF16),
        compiler_params=_cparams(("parallel", "parallel")),
        name="relayout_w_in",
    )(w_t)


def kernel(x, ffn1_norm, ffn1_w_gate, ffn1_w_up, ffn1_w_down, mix_norm, w_in, gdn_conv, gdn_a_log, gdn_dt_bias, gdn_out_norm, nsa_cmp_pe_k, nsa_cmp_pe_v, nsa_cmp_k_w1, nsa_cmp_k_w2, nsa_cmp_v_w1, nsa_cmp_v_w2, hgrn_lb_logits, hgrn_out_norm, w_proj_a, w_proj_b, w_proj_c, w_out, ffn2_norm, ffn2_w_gate, ffn2_w_up, ffn2_w_down, final_norm):
    B, S, D = x.shape
    L = w_in.shape[0]
    T = B * S
    row = lambda v: v.reshape(1, -1).astype(F32)
    bf = lambda w: w.astype(BF16)
    small_row = lambda v: jnp.zeros((1, DH), F32).at[0, SM_A:SM_A + HEADS].set(v.astype(F32))

    lb_p = jax.nn.softmax(hgrn_lb_logits.astype(F32), axis=0)
    lower_bounds = jnp.cumsum(lb_p, axis=0) - lb_p[0]
    fin = row(final_norm)
    f1 = (ffn1_w_gate, ffn1_w_up, ffn1_w_down)
    f2 = (ffn2_w_gate, ffn2_w_up, ffn2_w_down)
    w_in_r = _relayout_w_in(jnp.swapaxes(w_in.astype(F32), 1, 2))
    projs = (bf(w_proj_a), bf(w_proj_b), bf(w_proj_c), bf(w_out))

    xf = x.reshape(T, D)
    for l in range(L):
        xf = _ffn(xf, row(ffn1_norm[l]), *f1, fin, l, False)

        u, sm = _inproj(xf, row(mix_norm[l]), w_in_r, small_row(gdn_a_log[l]), small_row(gdn_dt_bias[l]), l)
        u3 = u.reshape(B, S, N_U)
        sm3 = sm.reshape(B, S, DH)
        ya = _gdn(u3, sm3, gdn_conv[l].astype(F32), row(gdn_out_norm[l]))
        yb = _nsa(u3, sm3, nsa_cmp_pe_k[l].astype(F32), nsa_cmp_pe_v[l].astype(F32),
                  nsa_cmp_k_w1[l], nsa_cmp_k_w2[l], nsa_cmp_v_w1[l], nsa_cmp_v_w2[l])
        oc = _hgrn(u3, lower_bounds[l].reshape(HEADS, 1, DH))
        xf = _merge(xf, u, ya.reshape(T, -1), yb.reshape(T, -1), oc.reshape(T, -1), row(hgrn_out_norm[l]),
                    *projs, l)

        xf = _ffn(xf, row(ffn2_norm[l]), *f2, fin, l, l == L - 1)
    return xf.reshape(B, S, D)
```

```python
import functools

import jax
import jax.numpy as jnp
from jax import lax
from jax.experimental import pallas as pl
from jax.experimental.pallas import tpu as pltpu

F32 = jnp.float32
BF16 = jnp.bfloat16

D_MODEL = 2048
D_FF = 5632
NORM_EPS = 1e-6
NEG_INF = -1e30
FORCE = 1e6
LOG_FLOOR = 1e-30

HEADS = 8
DH = 128
CHUNK = 64
GDN_CONV = 4
GDN_HPS = 4
NSA_GROUPS = 2
NSA_REP = HEADS // NSA_GROUPS
CMP_LEN = 32
CMP_STRIDE = 16
SLC_LEN = 64
SLC_SHIFT = 6
SLC_TOPN = 8
WIN_LEN = 512
NSA_TQ = 128
NSA_KEY_STEP = NSA_TQ

OFF_GQ, OFF_GK, OFF_GV, OFF_GG = 0, 1024, 2048, 3072
OFF_NQ, OFF_NKV, OFF_SM = 4096, 5120, 6656
OFF_HQ, OFF_HF, OFF_HI, OFF_HG = 7168, 8192, 9216, 10240
OFF_MG = 11264
N_U = 17408
SM_BETA, SM_A, SM_NGATE = 0, 8, 16

V7X_VMEM_BYTES = 64 * 1024 * 1024
VMEM_LIMIT = V7X_VMEM_BYTES * 7 // 8

FFN_TM, FFN_TF = 1024, 256
INPROJ_TM, INPROJ_TN = 1024, 1024
MERGE_TM, MERGE_TK = 1024, 512
RELAYOUT_CB = 128


def _cparams(sem):
    return pltpu.CompilerParams(dimension_semantics=sem, vmem_limit_bytes=VMEM_LIMIT)


def _sigmoid(x):
    return 1.0 / (1.0 + jnp.exp(-x))


def _silu(x):
    return x * _sigmoid(x)


def _rms(x, g):
    return x * lax.rsqrt(jnp.mean(x * x, axis=-1, keepdims=True) + NORM_EPS) * g


def _mm_nt(a, b):
    return lax.dot_general(a.astype(BF16), b.astype(BF16), (((1,), (1,)), ((), ())),
                           preferred_element_type=F32)


def _mm_tn(a, b):
    return lax.dot_general(a.astype(BF16), b.astype(BF16), (((0,), (0,)), ((), ())),
                           preferred_element_type=F32)


def _bmm(a, b):
    return jnp.einsum('cmk,ckn->cmn', a.astype(BF16), b.astype(BF16), preferred_element_type=F32)


def _bmm_nt(a, b):
    return jnp.einsum('cmk,cnk->cmn', a.astype(BF16), b.astype(BF16), preferred_element_type=F32)


def _bmm_tn(a, b):
    return jnp.einsum('ckm,ckn->cmn', a.astype(BF16), b.astype(BF16), preferred_element_type=F32)


def _split2(x):
    hi = x.astype(BF16)
    return hi, (x - hi.astype(F32)).astype(BF16)


def _mm_nt3(a, b):
    (ah, al), (bh, bl) = _split2(a), _split2(b)
    return _mm_nt(ah, bh) + (_mm_nt(ah, bl) + _mm_nt(al, bh))


def _mm3(a, b):
    (ah, al), (bh, bl) = _split2(a), _split2(b)
    d = lambda p, r: jnp.dot(p, r, preferred_element_type=F32)
    return d(ah, bh) + (d(ah, bl) + d(al, bh))


def _ffn_kernel(x_ref, g_ref, wg_ref, wu_ref, wd_ref, fg_ref, o_ref, h_ref, *, final_norm):
    f = pl.program_id(1)

    @pl.when(f == 0)
    def _():
        x = x_ref[...]
        h_ref[...] = _rms(x, g_ref[...]).astype(BF16)
        o_ref[...] = x

    h = h_ref[...]
    a = jnp.dot(h, wg_ref[...].astype(BF16), preferred_element_type=F32)
    b = jnp.dot(h, wu_ref[...].astype(BF16), preferred_element_type=F32)
    act = (_silu(a) * b * 0.5).astype(BF16)
    o_ref[...] += jnp.dot(act, wd_ref[...].astype(BF16), preferred_element_type=F32)

    if final_norm:
        @pl.when(f == pl.num_programs(1) - 1)
        def _():
            o_ref[...] = _rms(o_ref[...], fg_ref[...])


def _ffn(x, g, wg, wu, wd, fg, l, final_norm, tm=FFN_TM, tf=FFN_TF):
    T, D = x.shape
    F = wg.shape[2]
    return pl.pallas_call(
        functools.partial(_ffn_kernel, final_norm=final_norm),
        grid=(T // tm, F // tf),
        in_specs=[
            pl.BlockSpec((tm, D), lambda i, f: (i, 0)),
            pl.BlockSpec((1, D), lambda i, f: (0, 0)),
            pl.BlockSpec((None, D, tf), lambda i, f: (l, 0, f)),
            pl.BlockSpec((None, D, tf), lambda i, f: (l, 0, f)),
            pl.BlockSpec((None, tf, D), lambda i, f: (l, f, 0)),
            pl.BlockSpec((1, D), lambda i, f: (0, 0)),
        ],
        out_specs=pl.BlockSpec((tm, D), lambda i, f: (i, 0)),
        out_shape=jax.ShapeDtypeStruct((T, D), F32),
        scratch_shapes=[pltpu.VMEM((tm, D), BF16)],
        compiler_params=_cparams(("parallel", "arbitrary")),
        name="ffn",
    )(x, g, wg, wu, wd, fg)


def _inproj_kernel(x_ref, g_ref, w_ref, alog_ref, dtb_ref, o_ref, sm_ref, h_ref, *, tn):
    n = pl.program_id(1)

    @pl.when(n == 0)
    def _():
        h_ref[...] = _rms(x_ref[...], g_ref[...]).astype(BF16)

    acc = jnp.dot(h_ref[...], w_ref[...], preferred_element_type=F32)
    o_ref[...] = acc.astype(o_ref.dtype)

    @pl.when(n == OFF_SM // tn)
    def _():
        sm = acc[:, OFF_SM % tn:OFF_SM % tn + DH]
        lane = lax.broadcasted_iota(jnp.int32, (1, DH), 1)
        z = sm + dtb_ref[...]
        softplus = jnp.maximum(z, 0.0) + jnp.log1p(jnp.exp(-jnp.abs(z)))
        sm_ref[...] = jnp.where((lane >= SM_A) & (lane < SM_A + HEADS),
                                -jnp.exp(alog_ref[...]) * softplus, _sigmoid(sm))


def _inproj(x, g, w, a_log_row, dt_bias_row, l, tm=INPROJ_TM, tn=INPROJ_TN):
    T, D = x.shape
    N = w.shape[2]
    return pl.pallas_call(
        functools.partial(_inproj_kernel, tn=tn),
        grid=(T // tm, N // tn),
        in_specs=[
            pl.BlockSpec((tm, D), lambda i, n: (i, 0)),
            pl.BlockSpec((1, D), lambda i, n: (0, 0)),
            pl.BlockSpec((None, D, tn), lambda i, n: (l, 0, n)),
            pl.BlockSpec((1, DH), lambda i, n: (0, 0)),
            pl.BlockSpec((1, DH), lambda i, n: (0, 0)),
        ],
        out_specs=[pl.BlockSpec((tm, tn), lambda i, n: (i, n)),
                   pl.BlockSpec((tm, DH), lambda i, n: (i, 0))],
        out_shape=[jax.ShapeDtypeStruct((T, N), BF16), jax.ShapeDtypeStruct((T, DH), F32)],
        scratch_shapes=[pltpu.VMEM((tm, D), BF16)],
        compiler_params=_cparams(("parallel", "arbitrary")),
        name="in_proj",
    )(x, g, w, a_log_row, dt_bias_row)


def _gdn_kernel(q_ref, k_ref, v_ref, gate_ref, sm_ref, cq_ref, ck_ref, cv_ref, ng_ref,
                o_ref, kwt_s, bt_s, sp_s, eg_s, o0_s, qe_s):
    hp = pl.program_id(1)
    S = q_ref.shape[0]
    n = S // CHUNK
    C = CHUNK
    row8 = lax.broadcasted_iota(jnp.int32, (8, DH), 0)
    ti = lax.broadcasted_iota(jnp.int32, (1, C, C), 1)
    si = lax.broadcasted_iota(jnp.int32, (1, C, C), 2)
    lane = lax.broadcasted_iota(jnp.int32, (1, 1, DH), 2)
    causal = ti >= si
    strict = ti > si
    tril = jnp.broadcast_to(jnp.where(causal, 1.0, 0.0).astype(BF16), (n, C, C))
    eye = jnp.where(ti == si, 1.0, 0.0)

    def conv_silu(x, w):
        taps = [w[GDN_CONV - 1 - j:GDN_CONV - j] for j in range(GDN_CONV)]
        y = x * taps[0]
        for j in range(1, GDN_CONV):
            y = y + pltpu.roll(x, j, axis=0) * taps[j]
        x8 = x[0:8]
        y8 = x8 * taps[0]
        for j in range(1, GDN_CONV):
            y8 = y8 + jnp.where(row8 >= j, pltpu.roll(x8, j, axis=0), 0.0) * taps[j]
        return _silu(jnp.concatenate([y8, y[8:]], axis=0))

    def l2n(x):
        return x * lax.rsqrt(jnp.sum(x * x, axis=-1, keepdims=True) + 1e-6)

    def prepare(i):
        cols = slice(i * DH, (i + 1) * DH)
        h = hp * GDN_HPS + i
        q = l2n(conv_silu(q_ref[:, cols].astype(F32), cq_ref[:, cols])) * (DH ** -0.5)
        k = l2n(conv_silu(k_ref[:, cols].astype(F32), ck_ref[:, cols]))
        v = conv_silu(v_ref[:, cols].astype(F32), cv_ref[:, cols])
        q3, k3, v3 = (t.reshape(n, C, DH) for t in (q, k, v))

        sm3 = sm_ref[...].reshape(n, C, DH)
        beta3 = jnp.sum(jnp.where(lane == SM_BETA + h, sm3, 0.0), axis=-1, keepdims=True)
        g3 = jnp.sum(jnp.where(lane == SM_A + h, sm3, 0.0), axis=-1, keepdims=True)

        gc3 = sum(_bmm(tril, p) for p in _split2(jnp.broadcast_to(g3, (n, C, DH))))
        gc_row = jnp.swapaxes(gc3, 1, 2)[:, :C, :]
        decay = jnp.where(causal, jnp.exp(jnp.where(causal, gc3[:, :, :C] - gc_row, 0.0)), 0.0)

        kb3 = k3 * beta3
        m = jnp.where(strict, _bmm_nt(kb3, k3) * decay, 0.0)

        def lower_left(blk):
            half = blk // 2
            pair = ((ti & (blk - 1)) >= half) & ((si & (blk - 1)) < half)
            if blk < C:
                pair = pair & ((ti & -blk) == (si & -blk))
            return jnp.where(pair, m, 0.0)

        x = eye - lower_left(2)
        for blk in (4, 8, 16, 32, 64):
            x = x - _bmm(_bmm(x, lower_left(blk)), x)
        egc = jnp.exp(gc3)
        uw = _bmm(x, jnp.concatenate([v3 * beta3, kb3 * egc], axis=-1))
        attn = jnp.where(causal, _bmm_nt(q3, k3) * decay, 0.0)
        g_last = gc3[:, C - 1:C, :]
        kg = k3 * jnp.exp(g_last - gc3)
        au = _bmm(attn, uw)
        o0_s[i] = au[:, :, :DH]
        qe_s[i] = (q3 * egc - au[:, :, DH:]).astype(BF16)
        kwb = _bmm_tn(uw, kg)
        bt_s[i] = kwb[:, :DH, :]
        kwt_s[i] = kwb[:, DH:, :].astype(BF16)
        eg_s[i] = jnp.exp(g_last)

    for i in range(GDN_HPS):
        prepare(i)

    def body(c, sts):
        nxt = []
        for i, st in enumerate(sts):
            stb = st.astype(BF16)
            sp_s[i, c] = stb
            nxt.append(st * eg_s[i, c] - jnp.dot(stb, kwt_s[i, c], preferred_element_type=F32) + bt_s[i, c])
        return tuple(nxt)

    lax.fori_loop(0, n, body, tuple(jnp.zeros((DH, DH), F32) for _ in range(GDN_HPS)))

    for i in range(GDN_HPS):
        cols = slice(i * DH, (i + 1) * DH)
        o = (o0_s[i] + _bmm_nt(qe_s[i], sp_s[i])).reshape(S, DH)
        o_ref[:, cols] = (_rms(o, ng_ref[...]) * _silu(gate_ref[:, cols].astype(F32))).astype(o_ref.dtype)


def _gdn(u3, sm3, conv_w, norm_g):
    B, S, _ = u3.shape
    n = S // CHUNK
    W = GDN_HPS * DH

    def col(off):
        return pl.BlockSpec((None, S, W), lambda b, h: (b, 0, off // W + h))

    def cw(off):
        return pl.BlockSpec((GDN_CONV, W), lambda b, h: (0, off // W + h))

    return pl.pallas_call(
        _gdn_kernel,
        grid=(B, HEADS // GDN_HPS),
        in_specs=[
            col(OFF_GQ), col(OFF_GK), col(OFF_GV), col(OFF_GG),
            pl.BlockSpec((None, S, DH), lambda b, h: (b, 0, 0)),
            cw(OFF_GQ), cw(OFF_GK), cw(OFF_GV),
            pl.BlockSpec((1, DH), lambda b, h: (0, 0)),
        ],
        out_specs=pl.BlockSpec((None, S, W), lambda b, h: (b, 0, h)),
        out_shape=jax.ShapeDtypeStruct((B, S, HEADS * DH), BF16),
        scratch_shapes=[
            pltpu.VMEM((GDN_HPS, n, DH, DH), BF16), pltpu.VMEM((GDN_HPS, n, DH, DH), F32),
            pltpu.VMEM((GDN_HPS, n, DH, DH), BF16), pltpu.VMEM((GDN_HPS, n, 1, DH), F32),
            pltpu.VMEM((GDN_HPS, n, CHUNK, DH), F32), pltpu.VMEM((GDN_HPS, n, CHUNK, DH), BF16),
        ],
        compiler_params=_cparams(("parallel", "arbitrary")),
        name="gdn",
    )(u3, u3, u3, u3, sm3, conv_w, conv_w, conv_w, norm_g)


def _hgrn_kernel(q_ref, f_ref, i_ref, lb_ref, o_ref, kv_s, sp_s, eb_s):
    S = q_ref.shape[0]
    C = CHUNK
    n = S // C
    lb = lb_ref[...]
    q3 = _silu(q_ref[...].astype(F32)).reshape(n, C, DH)
    f = lb + (1.0 - lb) * _sigmoid(f_ref[...].astype(F32))
    logf = jnp.log(jnp.maximum(f, LOG_FLOOR))
    k3 = (1.0 - f).reshape(n, C, DH)
    v3 = i_ref[...].reshape(n, C, DH)

    ti = lax.broadcasted_iota(jnp.int32, (1, C, C), 1)
    si = lax.broadcasted_iota(jnp.int32, (1, C, C), 2)
    row = lax.broadcasted_iota(jnp.int32, (1, C, DH), 1)
    tril = jnp.broadcast_to(jnp.where(ti >= si, 1.0, 0.0).astype(BF16), (n, C, C))
    b3 = sum(_bmm(tril, p) for p in _split2(logf.reshape(n, C, DH)))
    b = b3.reshape(S, DH)

    def shifted(t, shift):
        return pltpu.roll(t.reshape(S, DH), shift, axis=0).reshape(n, C, DH)

    def mid_rows(blk):
        half = blk // 2
        if blk >= 8:
            t4 = b.reshape(S // blk, blk, DH)
            return jnp.broadcast_to(t4[:, half:half + 1, :], t4.shape).reshape(n, C, DH)
        if blk == 4:
            d = jnp.where((row & 1) == 1, shifted(b3, 1), b3)
            return jnp.where((row & 3) < 2, shifted(d, S - 2), d)
        return jnp.where((row & 1) == 0, shifted(b3, S - 1), b3)

    a = jnp.where(ti == si, jnp.sum(q3 * k3, axis=-1, keepdims=True), 0.0)
    for blk in (64, 32, 16, 8, 4, 2):
        half = blk // 2
        upper = (row & (blk - 1)) >= half
        x = jnp.where(upper, q3, k3) * jnp.exp(-jnp.abs(b3 - mid_rows(blk)))
        pair = ((ti & (blk - 1)) >= half) & ((si & (blk - 1)) < half)
        if blk < C:
            pair = pair & ((ti & -blk) == (si & -blk))
        a = jnp.where(pair, _bmm_nt(x, x), a)

    o_intra = _bmm(a, v3)
    b_last = b3[:, C - 1:C, :]
    kv_s[...] = _bmm_tn(v3, k3 * jnp.exp(b_last - b3))
    eb_s[...] = jnp.exp(b_last)

    def body(c, st):
        sp_s[c] = st.astype(BF16)
        return st * eb_s[c] + kv_s[c]

    lax.fori_loop(0, n, body, jnp.zeros((DH, DH), F32))
    o_ref[...] = (o_intra + _bmm_nt(q3 * jnp.exp(b3), sp_s[...])).reshape(S, DH).astype(o_ref.dtype)


def _hgrn(u3, lb_b):
    B, S, _ = u3.shape
    n = S // CHUNK

    def col(off):
        return pl.BlockSpec((None, S, DH), lambda b, h: (b, 0, off // DH + h))

    return pl.pallas_call(
        _hgrn_kernel,
        grid=(B, HEADS),
        in_specs=[col(OFF_HQ), col(OFF_HF), col(OFF_HI),
                  pl.BlockSpec((None, 1, DH), lambda b, h: (h, 0, 0))],
        out_specs=pl.BlockSpec((None, S, DH), lambda b, h: (b, 0, h)),
        out_shape=jax.ShapeDtypeStruct((B, S, HEADS * DH), BF16),
        scratch_shapes=[
            pltpu.VMEM((n, DH, DH), F32), pltpu.VMEM((n, DH, DH), BF16), pltpu.VMEM((n, 1, DH), F32),
        ],
        compiler_params=_cparams(("parallel", "arbitrary")),
        name="hgrn2",
    )(u3, u3, u3, lb_b)


def _nsa_kernel(q_ref, kc_ref, vc_ref, ks_ref, vs_ref, kw_ref, vw_ref, sm_ref,
                pek_ref, pev_ref, wk1_ref, wk2_ref, wv1_ref, wv2_ref,
                o_ref, x_s, kc_s, vc_s, kw_s, vt_s, vwt_s, oslc_s):
    g = pl.program_id(1)
    qi = pl.program_id(2)
    S = ks_ref.shape[0]
    TQ = NSA_TQ
    R = NSA_REP
    NC = S // CMP_STRIDE
    NB = S // SLC_LEN

    @pl.when(qi == 0)
    def _():
        def compress(x_ref, pe_ref, w1_ref, w2_ref):
            x_s[...] = x_ref[...].astype(F32)
            a = jnp.zeros((NC, DH), F32)
            b = jnp.zeros((NC, DH), F32)
            for j in range(CMP_STRIDE):
                xj = x_s[pl.ds(j, NC, stride=CMP_STRIDE), :]
                jb = CMP_STRIDE + j
                a = a + _mm3(xj + pe_ref[j:j + 1, :], w1_ref[j * DH:(j + 1) * DH, :])
                b = b + _mm3(xj + pe_ref[jb:jb + 1, :], w1_ref[jb * DH:(jb + 1) * DH, :])
            pre = a + pltpu.roll(b, NC - 1, axis=0)
            return _mm3(_silu(pre), w2_ref[...])

        kc_s[...] = compress(kc_ref, pek_ref, wk1_ref, wk2_ref)
        vc_s[...] = compress(vc_ref, pev_ref, wv1_ref, wv2_ref).astype(BF16)
        kw_s[0:WIN_LEN, :] = jnp.zeros((WIN_LEN, DH), BF16)
        kw_s[WIN_LEN:, :] = kw_ref[...]
        vt_s[...] = vs_ref[...].astype(F32).T.astype(BF16)
        vw_t = vw_ref[...].astype(F32).T.astype(BF16)
        for j in range(WIN_LEN // TQ):
            vwt_s[j] = jnp.zeros((DH, TQ), BF16)
        for j in range(S // TQ):
            vwt_s[WIN_LEN // TQ + j] = vw_t[:, j * TQ:(j + 1) * TQ]

    t0 = qi * TQ
    qb = q_ref[...].astype(F32) * (DH ** -0.5)
    q4 = jnp.concatenate([qb[:, r * DH:(r + 1) * DH] for r in range(R)], axis=0)
    q4b = q4.astype(BF16)

    def iota(shape, dim):
        return lax.broadcasted_iota(jnp.int32, shape, dim)

    sc = _mm_nt3(q4, kc_s[...])
    n_id = iota((R * TQ, NC), 1)
    t_c = t0 + (iota((R * TQ, NC), 0) & (TQ - 1))
    ok_c = (n_id * CMP_STRIDE + (CMP_LEN - 1) <= t_c) & (n_id < NC - 1)
    m_c = jnp.max(jnp.where(ok_c, sc, NEG_INF), axis=-1, keepdims=True)
    e_c = jnp.where(ok_c, jnp.exp(sc - m_c), 0.0)
    den_c = jnp.sum(e_c, axis=-1, keepdims=True)
    p_c = e_c / jnp.where(den_c > 0.0, den_c, 1.0)
    o_cmp = jnp.dot(p_c.astype(BF16), vc_s[...], preferred_element_type=F32)

    p_sum = p_c[0:TQ]
    for r in range(1, R):
        p_sum = p_sum + p_c[r * TQ:(r + 1) * TQ]
    js = iota((NB, NC), 0) * SLC_LEN
    cn = iota((NB, NC), 1) * CMP_STRIDE
    overlap_t = jnp.where((cn < js + SLC_LEN) & (cn + CMP_LEN > js), 1.0, 0.0).astype(BF16)
    imp = sum(_mm_nt(overlap_t, p) for p in _split2(p_sum))
    blk = iota((NB, TQ), 0)
    cur = (t0 + iota((NB, TQ), 1)) >> SLC_SHIFT
    forced = (blk == 0) | (blk == cur) | (blk == cur - 1)
    work = jnp.where(blk > cur, -FORCE, jnp.where(forced, FORCE, imp))
    blkf = blk.astype(F32)
    sel_bias = jnp.full((NB, TQ), NEG_INF, F32)
    for _ in range(SLC_TOPN):
        mx = jnp.max(work, axis=0, keepdims=True)
        idx = jnp.min(jnp.where(work == mx, blkf, float(NB)), axis=0, keepdims=True)
        hit = blkf == idx
        sel_bias = jnp.where(hit, 0.0, sel_bias)
        work = jnp.where(hit, -jnp.inf, work)
    sel_bias = sel_bias.astype(BF16)

    def selected(ne):
        expand = jnp.where((iota((NB, ne), 1) >> SLC_SHIFT) == iota((NB, ne), 0), 1.0, 0.0).astype(BF16)
        bias = _mm_tn(sel_bias, expand)
        tail = jnp.where(iota((TQ, TQ), 1) <= iota((TQ, TQ), 0), bias[:, ne - TQ:], NEG_INF)
        bias = tail if ne == TQ else jnp.concatenate([bias[:, :ne - TQ], tail], axis=1)
        oslc_s[...] = attend(ks_ref[0:ne, :], vt_s[:, 0:ne], bias)

    def attend(k, v_t, bias):
        nk = k.shape[0]
        s = _mm_nt(q4b, k).reshape(R, TQ, nk) + bias[None]
        e = jnp.exp(s - jnp.max(s, axis=-1, keepdims=True))
        den = jnp.sum(e, axis=-1, keepdims=True)
        o_t = _mm_nt(v_t, e.reshape(R * TQ, nk))
        return jnp.concatenate([o_t[:, r * TQ:(r + 1) * TQ].T / den[r] for r in range(R)], axis=0)

    for vi in range(S // NSA_KEY_STEP):
        pl.when((t0 >= vi * NSA_KEY_STEP) & (t0 < (vi + 1) * NSA_KEY_STEP))(
            functools.partial(selected, (vi + 1) * NSA_KEY_STEP))

    span = WIN_LEN + TQ
    kw = kw_s[pl.ds(pl.multiple_of(t0, TQ), span), :]
    vw_t = jnp.concatenate([vwt_s[qi + j] for j in range(span // TQ)], axis=1)
    rel = iota((TQ, span), 1) - WIN_LEN - iota((TQ, span), 0)
    ok_w = (rel <= 0) & (rel > -WIN_LEN) & (t0 - WIN_LEN + iota((TQ, span), 1) >= 0)
    o_win = attend(kw, vw_t, jnp.where(ok_w, 0.0, NEG_INF))

    sg = sm_ref[...]
    sl = iota((1, DH), 1)
    for r in range(R):
        rows = slice(r * TQ, (r + 1) * TQ)
        base = SM_NGATE + (g * R + r) * 3
        gate = [jnp.sum(jnp.where(sl == base + i, sg, 0.0), axis=-1, keepdims=True) for i in range(3)]
        o_ref[:, r * DH:(r + 1) * DH] = (gate[0] * o_cmp[rows] + gate[1] * oslc_s[rows, :]
                                         + gate[2] * o_win[rows]).astype(o_ref.dtype)


def _nsa(u3, sm3, pe_k, pe_v, wk1, wk2, wv1, wv2):
    B, S, _ = u3.shape
    G, R, TQ = NSA_GROUPS, NSA_REP, NSA_TQ
    NC = S // CMP_STRIDE

    def kv(i):
        return pl.BlockSpec((None, S, DH), lambda b, g, t: (b, 0, OFF_NKV // DH + i * G + g))

    def full(shape):
        return pl.BlockSpec(shape, lambda b, g, t: (0,) * len(shape))

    return pl.pallas_call(
        _nsa_kernel,
        grid=(B, G, S // TQ),
        in_specs=[
            pl.BlockSpec((None, TQ, R * DH), lambda b, g, t: (b, t, OFF_NQ // (R * DH) + g)),
            kv(0), kv(1), kv(2), kv(3), kv(4), kv(5),
            pl.BlockSpec((None, TQ, DH), lambda b, g, t: (b, t, 0)),
            full((CMP_LEN, DH)), full((CMP_LEN, DH)),
            full((CMP_LEN * DH, DH)), full((DH, DH)), full((CMP_LEN * DH, DH)), full((DH, DH)),
        ],
        out_specs=pl.BlockSpec((None, TQ, R * DH), lambda b, g, t: (b, t, g)),
        out_shape=jax.ShapeDtypeStruct((B, S, HEADS * DH), BF16),
        scratch_shapes=[
            pltpu.VMEM((S, DH), F32),
            pltpu.VMEM((NC, DH), F32), pltpu.VMEM((NC, DH), BF16),
            pltpu.VMEM((S + WIN_LEN, DH), BF16), pltpu.VMEM((DH, S), BF16),
            pltpu.VMEM(((S + WIN_LEN) // TQ, DH, TQ), BF16),
            pltpu.VMEM((R * TQ, DH), F32),
        ],
        compiler_params=_cparams(("parallel", "parallel", "arbitrary")),
        name="nsa",
    )(u3, u3, u3, u3, u3, u3, u3, sm3, pe_k, pe_v, wk1, wk2, wv1, wv2)


def _merge_kernel(x_ref, ya_ref, yb_ref, oc_ref, hg_ref, hn_ref, ga_ref, gb_ref, gc_ref,
                  pa_ref, pb_ref, pc_ref, wo_ref, o_ref, yc_s, mg_s, *, nk):
    c = pl.program_id(1)
    tk = o_ref.shape[1]

    @pl.when(c == 0)
    def _():
        yc_s[...] = (_rms(oc_ref[...].astype(F32), hn_ref[...]) * _silu(hg_ref[...].astype(F32))).astype(BF16)

    @pl.when(c < nk)
    def _():
        m = (_sigmoid(ga_ref[...].astype(F32)) * jnp.dot(ya_ref[...], pa_ref[...], preferred_element_type=F32)
             + _sigmoid(gb_ref[...].astype(F32)) * jnp.dot(yb_ref[...], pb_ref[...], preferred_element_type=F32)
             + _sigmoid(gc_ref[...].astype(F32)) * jnp.dot(yc_s[...], pc_ref[...], preferred_element_type=F32))
        mg_s[c] = m.astype(BF16)

    @pl.when(c >= nk)
    def _():
        acc = x_ref[...]
        for j in range(nk):
            acc = acc + jnp.dot(mg_s[j], wo_ref[j * tk:(j + 1) * tk, :], preferred_element_type=F32)
        o_ref[...] = acc


def _merge(x, u, ya, yb, oc, hnorm, pa, pb, pc, wo, l, tm=MERGE_TM, tk=MERGE_TK):
    T, D = x.shape
    V = ya.shape[1]
    nk = D // tk
    first = lambda c: jnp.minimum(c, nk - 1)
    second = lambda c: jnp.maximum(c - nk, 0)

    def gate(i):
        return pl.BlockSpec((tm, tk), lambda m, c: (m, (OFF_MG + i * D) // tk + first(c)))

    rows = lambda w: pl.BlockSpec((tm, w), lambda m, c: (m, 0))
    proj = pl.BlockSpec((None, V, tk), lambda m, c: (l, 0, first(c)))
    return pl.pallas_call(
        functools.partial(_merge_kernel, nk=nk),
        grid=(T // tm, 2 * nk),
        in_specs=[
            pl.BlockSpec((tm, tk), lambda m, c: (m, second(c))),
            rows(V), rows(V), rows(V),
            pl.BlockSpec((tm, V), lambda m, c: (m, OFF_HG // V)),
            pl.BlockSpec((1, V), lambda m, c: (0, 0)),
            gate(0), gate(1), gate(2),
            proj, proj, proj,
            pl.BlockSpec((None, D, tk), lambda m, c: (l, 0, second(c))),
        ],
        out_specs=pl.BlockSpec((tm, tk), lambda m, c: (m, second(c))),
        out_shape=jax.ShapeDtypeStruct((T, D), F32),
        scratch_shapes=[pltpu.VMEM((tm, V), BF16), pltpu.VMEM((nk, tm, tk), BF16)],
        compiler_params=_cparams(("parallel", "arbitrary")),
        name="merge",
    )(x, ya, yb, oc, u, hnorm, u, u, u, pa, pb, pc, wo)


SRC_QKV = (0, 3072)
SRC_BA = (3072, 3088)
SRC_MID = (3088, 6672)
SRC_NGATE = (6672, 6696)
SRC_TAIL = (6696, 16936)


RELAYOUT_ROWS = 512


def _relayout_kernel(wt_ref, o_ref):
    cb = wt_ref.shape[1]

    def put(dst, src):
        for s in range(src[0], src[1], RELAYOUT_ROWS):
            d = dst + s - src[0]
            o_ref[:, d:d + RELAYOUT_ROWS] = wt_ref[s:s + RELAYOUT_ROWS, :].T.astype(BF16)

    put(OFF_GQ, SRC_QKV)
    put(OFF_GG, SRC_MID)
    put(OFF_HQ, SRC_TAIL)
    small = jnp.concatenate([wt_ref[SRC_BA[0]:SRC_BA[1], :], wt_ref[SRC_NGATE[0]:SRC_NGATE[1], :],
                             jnp.zeros((DH - (SM_NGATE + 3 * HEADS), cb), F32)], axis=0)
    o_ref[:, OFF_SM:OFF_SM + DH] = small.T.astype(BF16)
    o_ref[:, OFF_SM + DH:OFF_HQ] = jnp.zeros((cb, OFF_HQ - OFF_SM - DH), BF16)


def _relayout_w_in(w_t, cb=RELAYOUT_CB):
    L, n_in, D = w_t.shape
    return pl.pallas_call(
        _relayout_kernel,
        grid=(L, D // cb),
        in_specs=[pl.BlockSpec((None, n_in, cb), lambda l, r: (l, 0, r))],
        out_specs=pl.BlockSpec((None, cb, N_U), lambda l, r: (l, r, 0)),
        out_shape=jax.ShapeDtypeStruct((L, D, N_U), BF16),
        compiler_params=_cparams(("parallel", "parallel")),
        name="relayout_w_in",
    )(w_t)


def kernel(x, ffn1_norm, ffn1_w_gate, ffn1_w_up, ffn1_w_down, mix_norm, w_in, gdn_conv, gdn_a_log, gdn_dt_bias, gdn_out_norm, nsa_cmp_pe_k, nsa_cmp_pe_v, nsa_cmp_k_w1, nsa_cmp_k_w2, nsa_cmp_v_w1, nsa_cmp_v_w2, hgrn_lb_logits, hgrn_out_norm, w_proj_a, w_proj_b, w_proj_c, w_out, ffn2_norm, ffn2_w_gate, ffn2_w_up, ffn2_w_down, final_norm):
    B, S, D = x.shape
    L = w_in.shape[0]
    T = B * S
    row = lambda v: v.reshape(1, -1).astype(F32)
    bf = lambda w: w.astype(BF16)
    small_row = lambda v: jnp.zeros((1, DH), F32).at[0, SM_A:SM_A + HEADS].set(v.astype(F32))

    lb_p = jax.nn.softmax(hgrn_lb_logits.astype(F32), axis=0)
    lower_bounds = jnp.cumsum(lb_p, axis=0) - lb_p[0]
    fin = row(final_norm)
    f1 = (ffn1_w_gate, ffn1_w_up, ffn1_w_down)
    f2 = (ffn2_w_gate, ffn2_w_up, ffn2_w_down)
    w_in_r = _relayout_w_in(jnp.swapaxes(w_in.astype(F32), 1, 2))
    projs = (bf(w_proj_a), bf(w_proj_b), bf(w_proj_c), bf(w_out))

    xf = x.reshape(T, D)
    for l in range(L):
        xf = _ffn(xf, row(ffn1_norm[l]), *f1, fin, l, False)

        u, sm = _inproj(xf, row(mix_norm[l]), w_in_r, small_row(gdn_a_log[l]), small_row(gdn_dt_bias[l]), l)
        u3 = u.reshape(B, S, N_U)
        sm3 = sm.reshape(B, S, DH)
        ya = _gdn(u3, sm3, gdn_conv[l].astype(F32), row(gdn_out_norm[l]))
        yb = _nsa(u3, sm3, nsa_cmp_pe_k[l].astype(F32), nsa_cmp_pe_v[l].astype(F32),
                  nsa_cmp_k_w1[l], nsa_cmp_k_w2[l], nsa_cmp_v_w1[l], nsa_cmp_v_w2[l])
        oc = _hgrn(u3, lower_bounds[l].reshape(HEADS, 1, DH))
        xf = _merge(xf, u, ya.reshape(T, -1), yb.reshape(T, -1), oc.reshape(T, -1), row(hgrn_out_norm[l]),
                    *projs, l)

        xf = _ffn(xf, row(ffn2_norm[l]), *f2, fin, l, l == L - 1)
    return xf.reshape(B, S, D)
```
